```python
import jax, jax.numpy as jnp
from jax import lax
import numpy as np

D_MODEL = 1024
BATCH = 8
SEQ = 2048
DEPTH = 4
DEC_BATCH = 128
DEC_SEQ = 8
PAST_LEN = 8192
PAGE_SIZE = 128

N_A = DEPTH // 2
N_B = DEPTH - N_A
CONV_WIDTH = 31
D_FF = 4 * D_MODEL
PLE_DIM = 256
N_HEADS = 8
QK_NOPE = 128
QK_ROPE = 64
V_HEAD = 128
KV_LORA = 512
Q_LORA = 768
ROPE_THETA = 10000.0
Q_BLOCK = 128
EPS = 1e-6
SM_SCALE = (QK_NOPE + QK_ROPE) ** -0.5

kernel_name = 'yoco_conformer_mla_decoder_step'


def rms_norm(x, g):
    xf = x.astype(jnp.float32)
    y = xf * lax.rsqrt(jnp.mean(xf * xf, axis=-1, keepdims=True) + EPS)
    return (y * g.astype(jnp.float32)).astype(x.dtype)


def layer_norm(x, g, b):
    xf = x.astype(jnp.float32)
    mu = jnp.mean(xf, axis=-1, keepdims=True)
    var = jnp.mean(jnp.square(xf - mu), axis=-1, keepdims=True)
    y = (xf - mu) * lax.rsqrt(var + EPS)
    return (y * g.astype(jnp.float32) + b.astype(jnp.float32)).astype(x.dtype)


def rope(x, pos):
    half = x.shape[-1] // 2
    freqs = 1.0 / (ROPE_THETA ** (jnp.arange(half, dtype=jnp.float32) / half))
    ang = pos.astype(jnp.float32)[:, None] * freqs[None, :]
    ang = ang.reshape(ang.shape[:1] + (1,) * (x.ndim - 3) + ang.shape[1:])
    cos, sin = jnp.cos(ang), jnp.sin(ang)
    xf = x.astype(jnp.float32)
    x1, x2 = xf[..., :half], xf[..., half:]
    return jnp.concatenate([x1 * cos - x2 * sin, x1 * sin + x2 * cos], axis=-1).astype(x.dtype)


def conformer_conv(h, conv_prev, w_pw1, w_dw, b_dw, ln_g, ln_b, w_pw2):
    a, gate = jnp.split(h @ w_pw1, 2, axis=-1)
    v = a * jax.nn.sigmoid(gate)
    full = jnp.concatenate([conv_prev.astype(v.dtype), v], axis=1)
    y = lax.conv_general_dilated(full, w_dw[:, None, :], window_strides=(1,), padding='VALID',
                                 dimension_numbers=('NWC', 'WIO', 'NWC'),
                                 feature_group_count=D_MODEL) + b_dw
    y = jax.nn.silu(layer_norm(y, ln_g, ln_b))
    return y @ w_pw2, full[:, -(CONV_WIDTH - 1):]


def sq_relu_mlp(h, w_up, w_down):
    return jnp.square(jax.nn.relu(h @ w_up)) @ w_down


def shared_kv(x, pos, kv_in_g, w_dkv, kv_norm_g):
    h = rms_norm(x, kv_in_g)
    ckr = h @ w_dkv
    c_kv = rms_norm(ckr[..., :KV_LORA], kv_norm_g)
    k_rope = rope(ckr[..., KV_LORA:], pos)
    return c_kv, k_rope


def mla_query(h, pos, w_dq, q_norm_g, w_uq, w_uk):
    B, T = h.shape[:2]
    cq = rms_norm(h @ w_dq, q_norm_g)
    q = (cq @ w_uq).reshape(B, T, N_HEADS, QK_NOPE + QK_ROPE)
    q_rope = rope(q[..., QK_NOPE:], pos)
    q_lat = jnp.einsum('bthn,chn->bthc', q[..., :QK_NOPE], w_uk)
    return q_lat, q_rope


def latent_scores(q_lat, q_rope, c, kr):
    s = jnp.einsum('bqhc,bkc->bhqk', q_lat, c, preferred_element_type=jnp.float32)
    s = s + jnp.einsum('bqhr,bkr->bhqk', q_rope, kr, preferred_element_type=jnp.float32)
    return s * SM_SCALE


def prompt_attention(q_lat, q_rope, c, kr, pos):
    B, T = q_lat.shape[:2]
    nb = T // Q_BLOCK

    def block(args):
        ql, qr, qp = args
        s = latent_scores(ql, qr, c, kr)
        s = jnp.where(pos[None, :] <= qp[:, None], s, -jnp.inf)
        p = jax.nn.softmax(s, axis=-1).astype(c.dtype)
        return jnp.einsum('bhqk,bkc->bqhc', p, c)

    ql = q_lat.reshape(B, nb, Q_BLOCK, N_HEADS, KV_LORA).swapaxes(0, 1)
    qr = q_rope.reshape(B, nb, Q_BLOCK, N_HEADS, QK_ROPE).swapaxes(0, 1)
    qp = pos.reshape(nb, Q_BLOCK)
    o = lax.map(block, (ql, qr, qp))
    return o.swapaxes(0, 1).reshape(B, T, N_HEADS, KV_LORA)


def sample_attention(q_lat, q_rope, past_c, past_kr, c, kr):
    T = q_lat.shape[1]
    P = past_c.shape[1]
    s_past = latent_scores(q_lat, q_rope, past_c, past_kr)
    s_new = latent_scores(q_lat, q_rope, c, kr)
    s_new = jnp.where(jnp.tril(jnp.ones((T, T), dtype=bool)), s_new, -jnp.inf)
    p = jax.nn.softmax(jnp.concatenate([s_past, s_new], axis=-1), axis=-1).astype(c.dtype)
    return (jnp.einsum('bhqk,bkc->bqhc', p[..., :P], past_c)
            + jnp.einsum('bhqk,bkc->bqhc', p[..., P:], c))


def mla_output(o_lat, w_uv, w_o):
    B, T = o_lat.shape[:2]
    o = jnp.einsum('bthc,chv->bthv', o_lat, w_uv).reshape(B, T, N_HEADS * V_HEAD)
    return o @ w_o


def run_group(x, ple, pos, conv_prev, past_c, past_kr, W):
    new_conv = []
    c_kv = None
    k_rope = None
    for i in range(DEPTH):
        h = rms_norm(x, W['norm_mix_pre'][i])
        if i < N_A:
            m, st = conformer_conv(h, conv_prev[i], W['conv_pw1'][i], W['conv_dw'][i], W['conv_dw_bias'][i],
                                   W['conv_ln_g'][i], W['conv_ln_b'][i], W['conv_pw2'][i])
            new_conv.append(st)
        else:
            j = i - N_A
            if j == 0:
                c_kv, k_rope = shared_kv(x, pos, W['kv_in_norm'], W['w_dkv'], W['kv_norm'])
            q_lat, q_rope = mla_query(h, pos, W['w_dq'][j], W['q_norm'][j], W['w_uq'][j], W['w_uk'])
            if past_c is None:
                o_lat = prompt_attention(q_lat, q_rope, c_kv, k_rope, pos)
            else:
                o_lat = sample_attention(q_lat, q_rope, past_c, past_kr, c_kv, k_rope)
            m = mla_output(o_lat, W['w_uv'], W['w_o'][j])
        x = x + rms_norm(m, W['norm_mix_post'][i])
        h = rms_norm(x, W['norm_ff_pre'][i])
        x = x + rms_norm(sq_relu_mlp(h, W['w_up'][i], W['w_down'][i]), W['norm_ff_post'][i])
        x = x + jax.nn.sigmoid(x @ W['ple_gate'][i]) * (ple[i] @ W['ple_proj'][i])
    return x, jnp.stack(new_conv), c_kv, k_rope


def setup_inputs(seed: int = 0) -> dict:
    key = jax.random.key(seed)
    ks = iter(jax.random.split(key, 48))

    def nrm(shape, scale):
        return jax.random.normal(next(ks), shape, jnp.float32) * scale

    def gain(shape):
        return 1.0 + nrm(shape, 0.05)

    n_pages = PAST_LEN // PAGE_SIZE
    n_used = DEC_BATCH * n_pages
    n_phys = n_used + n_used // 4
    page_table = jax.random.permutation(next(ks), n_phys)[:n_used].reshape(DEC_BATCH, n_pages).astype(jnp.int32)

    return {
        'x_prompt': nrm((BATCH, SEQ, D_MODEL), 1.0),
        'x_sample': nrm((DEC_BATCH, DEC_SEQ, D_MODEL), 1.0),
        'state_conv': nrm((N_A, DEC_BATCH, CONV_WIDTH - 1, D_MODEL), 1.0),
        'cache_kv_latent': nrm((n_phys, PAGE_SIZE, KV_LORA), 1.0),
        'cache_k_rope': nrm((n_phys, PAGE_SIZE, QK_ROPE), 1.0),
        'page_table': page_table,
        'p_prompt': nrm((DEPTH, BATCH, SEQ, PLE_DIM), 1.0),
        'p_sample': nrm((DEPTH, DEC_BATCH, DEC_SEQ, PLE_DIM), 1.0),
        'norm_mix_pre': gain((DEPTH, D_MODEL)),
        'norm_mix_post': gain((DEPTH, D_MODEL)),
        'norm_ff_pre': gain((DEPTH, D_MODEL)),
        'norm_ff_post': gain((DEPTH, D_MODEL)),
        'conv_pw1': nrm((N_A, D_MODEL, 2 * D_MODEL), D_MODEL ** -0.5),
        'conv_dw': nrm((N_A, CONV_WIDTH, D_MODEL), CONV_WIDTH ** -0.5),
        'conv_dw_bias': nrm((N_A, D_MODEL), 0.01),
        'conv_ln_g': gain((N_A, D_MODEL)),
        'conv_ln_b': nrm((N_A, D_MODEL), 0.01),
        'conv_pw2': nrm((N_A, D_MODEL, D_MODEL), D_MODEL ** -0.5),
        'kv_in_norm': gain((D_MODEL,)),
        'w_dkv': nrm((D_MODEL, KV_LORA + QK_ROPE), D_MODEL ** -0.5),
        'kv_norm': gain((KV_LORA,)),
        'w_uk': nrm((KV_LORA, N_HEADS, QK_NOPE), KV_LORA ** -0.5),
        'w_uv': nrm((KV_LORA, N_HEADS, V_HEAD), KV_LORA ** -0.5),
        'w_dq': nrm((N_B, D_MODEL, Q_LORA), D_MODEL ** -0.5),
        'q_norm': gain((N_B, Q_LORA)),
        'w_uq': nrm((N_B, Q_LORA, N_HEADS * (QK_NOPE + QK_ROPE)), Q_LORA ** -0.5),
        'w_o': nrm((N_B, N_HEADS * V_HEAD, D_MODEL), (N_HEADS * V_HEAD) ** -0.5),
        'w_up': nrm((DEPTH, D_MODEL, D_FF), D_MODEL ** -0.5),
        'w_down': nrm((DEPTH, D_FF, D_MODEL), D_FF ** -0.5),
        'ple_proj': nrm((DEPTH, PLE_DIM, D_MODEL), PLE_DIM ** -0.5),
        'ple_gate': nrm((DEPTH, D_MODEL, D_MODEL), D_MODEL ** -0.5),
    }


def reference(x_prompt, x_sample, state_conv, cache_kv_latent, cache_k_rope, page_table, p_prompt, p_sample,
              norm_mix_pre, norm_mix_post, norm_ff_pre, norm_ff_post,
              conv_pw1, conv_dw, conv_dw_bias, conv_ln_g, conv_ln_b, conv_pw2,
              kv_in_norm, w_dkv, kv_norm, w_uk, w_uv,
              w_dq, q_norm, w_uq, w_o,
              w_up, w_down, ple_proj, ple_gate):
    W = {
        'norm_mix_pre': norm_mix_pre, 'norm_mix_post': norm_mix_post,
        'norm_ff_pre': norm_ff_pre, 'norm_ff_post': norm_ff_post,
        'conv_pw1': conv_pw1, 'conv_dw': conv_dw, 'conv_dw_bias': conv_dw_bias,
        'conv_ln_g': conv_ln_g, 'conv_ln_b': conv_ln_b, 'conv_pw2': conv_pw2,
        'kv_in_norm': kv_in_norm, 'w_dkv': w_dkv, 'kv_norm': kv_norm, 'w_uk': w_uk, 'w_uv': w_uv,
        'w_dq': w_dq, 'q_norm': q_norm, 'w_uq': w_uq, 'w_o': w_o,
        'w_up': w_up, 'w_down': w_down, 'ple_proj': ple_proj, 'ple_gate': ple_gate,
    }
    B, T = x_prompt.shape[:2]
    conv0 = jnp.zeros((N_A, B, CONV_WIDTH - 1, D_MODEL), x_prompt.dtype)
    y_prompt, conv_p, c_p, kr_p = run_group(x_prompt, p_prompt, jnp.arange(T), conv0, None, None, W)

    DB, TS = x_sample.shape[:2]
    past_len = page_table.shape[1] * PAGE_SIZE
    past_c = cache_kv_latent[page_table].reshape(DB, past_len, KV_LORA)
    past_kr = cache_k_rope[page_table].reshape(DB, past_len, QK_ROPE)
    y_sample, conv_s, c_s, kr_s = run_group(x_sample, p_sample, past_len + jnp.arange(TS), state_conv,
                                            past_c, past_kr, W)
    return (y_prompt, y_sample, conv_p, conv_s, c_p, kr_p, c_s, kr_s)
```

```python
import functools

import jax
import jax.numpy as jnp
from jax import lax
from jax.experimental import pallas as pl
from jax.experimental.pallas import tpu as pltpu

F32 = jnp.float32
BF16 = jnp.bfloat16

EPS = 1e-6
ROPE_THETA = 10000.0
LANES = 128
VMEM_LIMIT = 56 * 1024 * 1024

CONV_PAD = 32
FF_CHUNK = 1024


def _rms(x, g):
    return x * lax.rsqrt(jnp.mean(x * x, axis=-1, keepdims=True) + EPS) * g


def _mm(a, w):
    return jnp.dot(a.astype(BF16), w, preferred_element_type=F32)


def _dot_nt(a, b):
    return lax.dot_general(a, b, (((1,), (1,)), ((), ())), preferred_element_type=F32)


def _rope_chunk(chunk, cos_t, sin_t):
    half = chunk.shape[-1] // 4
    lane = lax.broadcasted_iota(jnp.int32, chunk.shape, 1)
    swapped = jnp.where(lane < half,
                        pltpu.roll(chunk, LANES - half, 1),
                        pltpu.roll(chunk, half, 1))
    return chunk * cos_t + swapped * sin_t


def _ffn_ple(x, ple, g_ffpre, g_ffpost, wup, wdown, wgate, wproj):
    d_ff = wup.shape[1]
    h = _rms(x, g_ffpre).astype(BF16)
    acc = jnp.zeros_like(x)
    for c in range(d_ff // FF_CHUNK):
        u = jnp.dot(h, wup[:, c * FF_CHUNK:(c + 1) * FF_CHUNK], preferred_element_type=F32)
        u = jnp.square(jnp.maximum(u, 0.0)).astype(BF16)
        acc = acc + jnp.dot(u, wdown[c * FF_CHUNK:(c + 1) * FF_CHUNK, :],
                            preferred_element_type=F32)
    x = x + _rms(acc, g_ffpost)
    gate = jax.nn.sigmoid(_mm(x, wgate[...]))
    return x + gate * _mm(ple, wproj[...])


def _layer_norm_silu(y, g, b):
    mu = jnp.mean(y, axis=-1, keepdims=True)
    yc = y - mu
    var = jnp.mean(yc * yc, axis=-1, keepdims=True)
    z = yc * lax.rsqrt(var + EPS) * g + b
    return z * jax.nn.sigmoid(z)


def _conv_prompt_kernel(x_ref, ple_ref, prev_ref, gpre, gpost, gffpre, gffpost,
                        pw1, dw, dwb, lng, lnb, pw2, wup, wdown, wgate, wproj,
                        xo_ref, st_ref, buf, ybuf, *, tt, width):
    t = pl.program_id(1)
    d = x_ref.shape[-1]

    @pl.when(t == 0)
    def _():
        buf[0:CONV_PAD, :] = prev_ref[0]

    x = x_ref[0]
    h = _rms(x, gpre[...])
    ag = _mm(h, pw1[...])
    buf[CONV_PAD:CONV_PAD + tt, :] = ag[:, :d] * jax.nn.sigmoid(ag[:, d:])

    off = CONV_PAD - (width - 1)
    rc, lc = 64, 256
    for r in range(tt // rc):
        for c in range(d // lc):
            cs = slice(c * lc, (c + 1) * lc)
            acc = jnp.broadcast_to(dwb[:, cs], (rc, lc))
            for k in range(width):
                acc = acc + buf[r * rc + k + off:r * rc + k + off + rc, cs] * dw[k:k + 1, cs]
            ybuf[r * rc:(r + 1) * rc, cs] = acc

    y = _layer_norm_silu(ybuf[...], lng[...], lnb[...])
    x = x + _rms(_mm(y, pw2[...]), gpost[...])
    xo_ref[0] = _ffn_ple(x, ple_ref[...], gffpre[...], gffpost[...], wup, wdown, wgate, wproj)

    tail = buf[tt:tt + CONV_PAD, :]
    buf[0:CONV_PAD, :] = tail

    @pl.when(t == pl.num_programs(1) - 1)
    def _():
        st_ref[0] = tail


def _conv_sample_kernel(x_ref, ple_ref, prev_ref, gpre, gpost, gffpre, gffpost,
                        pw1, dw, dwb, lng, lnb, pw2, wup, wdown, wgate, wproj,
                        xo_ref, st_ref, full, ybuf, *, ts, width):
    d = x_ref.shape[-1]
    bb = prev_ref.shape[0]
    x = x_ref[...]
    h = _rms(x, gpre[...])
    ag = _mm(h, pw1[...])
    v = ag[:, :d] * jax.nn.sigmoid(ag[:, d:])
    full[:, 0:CONV_PAD, :] = prev_ref[...]
    full[:, CONV_PAD:CONV_PAD + ts, :] = v.reshape(bb, ts, d)

    off = CONV_PAD - (width - 1)
    sb = 4
    for s in range(bb // sb):
        acc = jnp.broadcast_to(dwb[...].reshape(1, 1, d), (sb, ts, d))
        for k in range(width):
            acc = acc + (full[s * sb:(s + 1) * sb, k + off:k + off + ts, :]
                         * dw[k:k + 1, :].reshape(1, 1, d))
        ybuf[s * sb:(s + 1) * sb, :, :] = acc

    y = _layer_norm_silu(ybuf[...].reshape(bb * ts, d), lng[...], lnb[...])
    x = x + _rms(_mm(y, pw2[...]), gpost[...])
    xo_ref[...] = _ffn_ple(x, ple_ref[...], gffpre[...], gffpost[...], wup, wdown, wgate, wproj)
    st_ref[...] = full[:, ts:ts + CONV_PAD, :]


def _kv_kernel(x_ref, cos_ref, sin_ref, gin, gkv, wdkv, *rest, kv_lora, rope_dim, n_heads, up):
    if up:
        wuk, wuv, c_ref, kr_ref, kcat_ref, v_ref = rest
    else:
        c_ref, kr_ref = rest
    h = _rms(x_ref[...], gin[...])
    ckr = _mm(h, wdkv[...])
    c = _rms(ckr[:, :kv_lora], gkv[...])
    c_ref[...] = c
    kr = _rope_chunk(ckr[:, kv_lora:kv_lora + LANES], cos_ref[...], sin_ref[...])
    kr_ref[...] = kr[:, :rope_dim]
    if up:
        cb = c.astype(BF16)
        kn = jnp.dot(cb, wuk[...], preferred_element_type=F32)
        v_ref[...] = jnp.dot(cb, wuv[...], preferred_element_type=F32).astype(BF16)
        krb = kr.astype(BF16)
        nope = kn.shape[1] // n_heads
        for hh in range(n_heads):
            base = hh * (nope + LANES)
            kcat_ref[:, base:base + nope] = kn[:, hh * nope:(hh + 1) * nope].astype(BF16)
            kcat_ref[:, base + nope:base + nope + LANES] = krb


def _q_kernel(x_ref, cos_ref, sin_ref, gpre, gq, wdq, wuq, *rest, n_heads, nope, scale, absorb):
    h = _rms(x_ref[...], gpre[...])
    cq = _rms(_mm(h, wdq[...]), gq[...])
    q = _mm(cq, wuq[...]) * scale
    cos_t = cos_ref[...]
    sin_t = sin_ref[...]
    if absorb:
        wukt, qlat_ref, qrope_ref = rest
        for hh in range(n_heads):
            qlat_ref[hh] = _mm(q[:, hh * nope:(hh + 1) * nope], wukt[hh])
            rb = n_heads * nope + hh * LANES
            qrope_ref[hh] = _rope_chunk(q[:, rb:rb + LANES], cos_t, sin_t)
    else:
        (qcat_ref,) = rest
        for hh in range(n_heads):
            base = hh * (nope + LANES)
            qcat_ref[:, base:base + nope] = q[:, hh * nope:(hh + 1) * nope].astype(BF16)
            rb = n_heads * nope + hh * LANES
            qcat_ref[:, base + nope:base + nope + LANES] = _rope_chunk(
                q[:, rb:rb + LANES], cos_t, sin_t).astype(BF16)


def _prompt_attn_kernel(q_ref, k_ref, v_ref, o_ref, *, tq):
    t = q_ref.shape[1]
    row = lax.broadcasted_iota(jnp.int32, (tq, tq), 0)
    col = lax.broadcasted_iota(jnp.int32, (tq, tq), 1)
    causal = col <= row
    for qi in range(t // tq):
        lo = qi * tq
        q = q_ref[0, lo:lo + tq, :]
        s_d = jnp.where(causal, _dot_nt(q, k_ref[0, lo:lo + tq, :]), -jnp.inf)
        m = jnp.max(s_d, axis=-1, keepdims=True)
        if qi > 0:
            s_o = _dot_nt(q, k_ref[0, 0:lo, :])
            m = jnp.maximum(m, jnp.max(s_o, axis=-1, keepdims=True))
        p_d = jnp.exp(s_d - m)
        l = jnp.sum(p_d, axis=-1, keepdims=True)
        o = jnp.dot(p_d.astype(BF16), v_ref[0, lo:lo + tq, :], preferred_element_type=F32)
        if qi > 0:
            p_o = jnp.exp(s_o - m)
            l = l + jnp.sum(p_o, axis=-1, keepdims=True)
            o = o + jnp.dot(p_o.astype(BF16), v_ref[0, 0:lo, :], preferred_element_type=F32)
        o_ref[0, lo:lo + tq, :] = (o / l).astype(o_ref.dtype)


def _sample_attn_kernel(pt_ref, qlat_ref, qrope_ref, cnew_ref, krnew_ref, *rest,
                        kp, ts, rope_dim):
    del pt_ref
    c_refs = rest[:kp]
    kr_refs = rest[kp:2 * kp]
    o_ref = rest[2 * kp]
    m_s, l_s, acc_s, qb_s, qrb_s, cn_s, krn_s = rest[2 * kp + 1:]
    j = pl.program_id(1)
    n_heads = qlat_ref.shape[0]
    rows = n_heads * ts
    page = c_refs[0].shape[1]

    @pl.when(j == 0)
    def _():
        m_s[...] = jnp.full(m_s.shape, -jnp.inf, F32)
        l_s[...] = jnp.zeros(l_s.shape, F32)
        acc_s[...] = jnp.zeros(acc_s.shape, F32)
        qb_s[...] = qlat_ref[...].reshape(rows, qlat_ref.shape[-1]).astype(BF16)
        qrb_s[...] = qrope_ref[...].reshape(rows, LANES)[:, :rope_dim].astype(BF16)

    qb = qb_s[...]
    qrb = qrb_s[...]

    def fold(s_list, c_list):
        m_old = m_s[...]
        m_new = m_old
        for s in s_list:
            m_new = jnp.maximum(m_new, jnp.max(s, axis=-1, keepdims=True))
        alpha = jnp.exp(m_old - m_new)
        l = alpha * l_s[...]
        acc = alpha * acc_s[...]
        for s, cb in zip(s_list, c_list):
            p = jnp.exp(s - m_new)
            l = l + jnp.sum(p, axis=-1, keepdims=True)
            acc = acc + jnp.dot(p.astype(BF16), cb, preferred_element_type=F32)
        m_s[...] = m_new
        l_s[...] = l
        acc_s[...] = acc

    s_list, c_list = [], []
    for i in range(kp):
        cb = c_refs[i][0].astype(BF16)
        krb = kr_refs[i][0].astype(BF16)
        s_list.append(_dot_nt(qb, cb) + _dot_nt(qrb, krb))
        c_list.append(cb)
    fold(s_list, c_list)

    @pl.when(j == pl.num_programs(1) - 1)
    def _():
        cn_s[...] = jnp.zeros(cn_s.shape, F32)
        krn_s[...] = jnp.zeros(krn_s.shape, F32)
        cn_s[0:ts, :] = cnew_ref[...]
        krn_s[0:ts, :] = krnew_ref[...]
        cb = cn_s[...].astype(BF16)
        krb = krn_s[...].astype(BF16)
        s = _dot_nt(qb, cb) + _dot_nt(qrb, krb)
        t_q = lax.broadcasted_iota(jnp.int32, (rows, page), 0) % ts
        t_k = lax.broadcasted_iota(jnp.int32, (rows, page), 1)
        s = jnp.where(t_k <= t_q, s, -jnp.inf)
        fold([s], [cb])
        o = acc_s[...] / l_s[...]
        o_ref[...] = o.reshape(o_ref.shape)


def _post_kernel(x_ref, o_ref, ple_ref, gpost, gffpre, gffpost, wo, *rest, latent):
    if latent:
        wuv, wup, wdown, wgate, wproj, xo_ref = rest
        n_heads, _, vh = wuv.shape
        m = jnp.zeros(x_ref.shape, F32)
        for hh in range(n_heads):
            oh = _mm(o_ref[hh], wuv[hh])
            m = m + _mm(oh, wo[hh * vh:(hh + 1) * vh, :])
    else:
        wup, wdown, wgate, wproj, xo_ref = rest
        m = jnp.dot(o_ref[...], wo[...], preferred_element_type=F32)
    x = x_ref[...] + _rms(m, gpost[...])
    xo_ref[...] = _ffn_ple(x, ple_ref[...], gffpre[...], gffpost[...], wup, wdown, wgate, wproj)


def _wspec(arr):
    nd = arr.ndim
    return pl.BlockSpec(arr.shape, lambda *_: (0,) * nd, pipeline_mode=pl.Buffered(1))


def _params(sem):
    return pltpu.CompilerParams(dimension_semantics=sem, vmem_limit_bytes=VMEM_LIMIT)


def _conv_weights(W, i):
    return [W['g_pre'][i], W['g_post'][i], W['g_ffpre'][i], W['g_ffpost'][i],
            W['pw1'][i], W['dw'][i], W['dwb'][i], W['lng'][i], W['lnb'][i], W['pw2'][i],
            W['w_up'][i], W['w_down'][i], W['ple_gate'][i], W['ple_proj'][i]]


def _conv_layer_prompt(x, ple, prev, W, i, tt=256):
    b, t, d = x.shape
    nt = t // tt
    ws = _conv_weights(W, i)
    width = W['dw'][i].shape[0]
    return pl.pallas_call(
        functools.partial(_conv_prompt_kernel, tt=tt, width=width),
        grid=(b, nt),
        in_specs=[pl.BlockSpec((1, tt, d), lambda bi, ti: (bi, ti, 0)),
                  pl.BlockSpec((tt, ple.shape[-1]), lambda bi, ti: (bi * nt + ti, 0)),
                  pl.BlockSpec((1, CONV_PAD, d), lambda bi, ti: (bi, 0, 0))]
                 + [_wspec(w) for w in ws],
        out_specs=[pl.BlockSpec((1, tt, d), lambda bi, ti: (bi, ti, 0)),
                   pl.BlockSpec((1, CONV_PAD, d), lambda bi, ti: (bi, 0, 0))],
        out_shape=[jax.ShapeDtypeStruct((b, t, d), F32),
                   jax.ShapeDtypeStruct((b, CONV_PAD, d), F32)],
        scratch_shapes=[pltpu.VMEM((CONV_PAD + tt, d), F32), pltpu.VMEM((tt, d), F32)],
        compiler_params=_params(("arbitrary", "arbitrary")),
        name=f"conv_layer_prompt_{i}",
    )(x, ple, prev, *ws)


def _conv_layer_sample(x, ple, prev, W, i, ts, bb=32):
    n, d = x.shape
    tm = bb * ts
    ws = _conv_weights(W, i)
    width = W['dw'][i].shape[0]
    return pl.pallas_call(
        functools.partial(_conv_sample_kernel, ts=ts, width=width),
        grid=(n // tm,),
        in_specs=[pl.BlockSpec((tm, d), lambda g: (g, 0)),
                  pl.BlockSpec((tm, ple.shape[-1]), lambda g: (g, 0)),
                  pl.BlockSpec((bb, CONV_PAD, d), lambda g: (g, 0, 0))]
                 + [_wspec(w) for w in ws],
        out_specs=[pl.BlockSpec((tm, d), lambda g: (g, 0)),
                   pl.BlockSpec((bb, CONV_PAD, d), lambda g: (g, 0, 0))],
        out_shape=[jax.ShapeDtypeStruct((n, d), F32),
                   jax.ShapeDtypeStruct((n // ts, CONV_PAD, d), F32)],
        scratch_shapes=[pltpu.VMEM((bb, CONV_PAD + ts, d), F32), pltpu.VMEM((bb, ts, d), F32)],
        compiler_params=_params(("arbitrary",)),
        name=f"conv_layer_sample_{i}",
    )(x, ple, prev, *ws)


def _table_spec(tab, tm):
    nblk = tab.shape[0] // tm
    return pl.BlockSpec((tm, LANES), lambda g: (g % nblk, 0))


def _kv_call(x, cos_t, sin_t, W, up, tm=256):
    n, d = x.shape
    kv_lora = W['g_kv'].shape[-1]
    rope_dim = W['rope_dim']
    n_heads = W['n_heads']
    ws = [W['g_kvin'], W['g_kv'], W['w_dkv']] + ([W['w_uk_flat'], W['w_uv_flat']] if up else [])
    out_specs = [pl.BlockSpec((tm, kv_lora), lambda g: (g, 0)),
                 pl.BlockSpec((tm, rope_dim), lambda g: (g, 0))]
    out_shape = [jax.ShapeDtypeStruct((n, kv_lora), F32),
                 jax.ShapeDtypeStruct((n, rope_dim), F32)]
    if up:
        kw = W['w_uk_flat'].shape[1] + n_heads * LANES
        vw = W['w_uv_flat'].shape[1]
        out_specs += [pl.BlockSpec((tm, kw), lambda g: (g, 0)),
                      pl.BlockSpec((tm, vw), lambda g: (g, 0))]
        out_shape += [jax.ShapeDtypeStruct((n, kw), BF16), jax.ShapeDtypeStruct((n, vw), BF16)]
    return pl.pallas_call(
        functools.partial(_kv_kernel, kv_lora=kv_lora, rope_dim=rope_dim, n_heads=n_heads, up=up),
        grid=(n // tm,),
        in_specs=[pl.BlockSpec((tm, d), lambda g: (g, 0)),
                  _table_spec(cos_t, tm), _table_spec(sin_t, tm)] + [_wspec(w) for w in ws],
        out_specs=out_specs,
        out_shape=out_shape,
        compiler_params=_params(("parallel",)),
        name="shared_kv_up" if up else "shared_kv",
    )(x, cos_t, sin_t, *ws)


def _q_call(x, cos_t, sin_t, W, i, j, absorb, tm=256):
    n, d = x.shape
    n_heads, nope = W['n_heads'], W['nope']
    ws = [W['g_pre'][i], W['g_q'][j], W['w_dq'][j], W['w_uq'][j]]
    if absorb:
        ws.append(W['w_ukt'])
        kv_lora = W['w_ukt'].shape[-1]
        out_specs = [pl.BlockSpec((n_heads, tm, kv_lora), lambda g: (0, g, 0)),
                     pl.BlockSpec((n_heads, tm, LANES), lambda g: (0, g, 0))]
        out_shape = [jax.ShapeDtypeStruct((n_heads, n, kv_lora), F32),
                     jax.ShapeDtypeStruct((n_heads, n, LANES), F32)]
    else:
        qw = n_heads * (nope + LANES)
        out_specs = [pl.BlockSpec((tm, qw), lambda g: (g, 0))]
        out_shape = [jax.ShapeDtypeStruct((n, qw), BF16)]
    return pl.pallas_call(
        functools.partial(_q_kernel, n_heads=n_heads, nope=nope, scale=W['sm_scale'],
                          absorb=absorb),
        grid=(n // tm,),
        in_specs=[pl.BlockSpec((tm, d), lambda g: (g, 0)),
                  _table_spec(cos_t, tm), _table_spec(sin_t, tm)] + [_wspec(w) for w in ws],
        out_specs=out_specs,
        out_shape=out_shape,
        compiler_params=_params(("parallel",)),
        name=f"mla_query_{'sample' if absorb else 'prompt'}_{j}",
    )(x, cos_t, sin_t, *ws)


def _prompt_attn(qcat, kcat, v, n_heads, tq=256):
    b, t, qw = qcat.shape
    hw = qw // n_heads
    vh = v.shape[-1] // n_heads
    return pl.pallas_call(
        functools.partial(_prompt_attn_kernel, tq=tq),
        grid=(b, n_heads),
        in_specs=[pl.BlockSpec((1, t, hw), lambda bi, hi: (bi, 0, hi)),
                  pl.BlockSpec((1, t, hw), lambda bi, hi: (bi, 0, hi)),
                  pl.BlockSpec((1, t, vh), lambda bi, hi: (bi, 0, hi))],
        out_specs=pl.BlockSpec((1, t, vh), lambda bi, hi: (bi, 0, hi)),
        out_shape=jax.ShapeDtypeStruct((b, t, n_heads * vh), BF16),
        compiler_params=_params(("parallel", "parallel")),
        name="prompt_attention",
    )(qcat, kcat, v)


def _sample_attn(qlat, qrope, cnew, krnew, cache_c, cache_kr, page_table, ts, kp=8):
    n_heads, n, kv_lora = qlat.shape
    db, n_pages = page_table.shape
    _, page, rope_dim = cache_kr.shape
    nchunk = n_pages // kp
    rows = n_heads * ts
    pt_flat = page_table.reshape(-1)

    def page_map(i):
        return lambda b, j, pt: (pt[b * n_pages + j * kp + i], 0, 0)

    in_specs = [pl.BlockSpec((n_heads, ts, kv_lora), lambda b, j, pt: (0, b, 0)),
                pl.BlockSpec((n_heads, ts, LANES), lambda b, j, pt: (0, b, 0)),
                pl.BlockSpec((ts, kv_lora), lambda b, j, pt: (b, 0)),
                pl.BlockSpec((ts, rope_dim), lambda b, j, pt: (b, 0))]
    in_specs += [pl.BlockSpec((1, page, kv_lora), page_map(i)) for i in range(kp)]
    in_specs += [pl.BlockSpec((1, page, rope_dim), page_map(i)) for i in range(kp)]
    grid_spec = pltpu.PrefetchScalarGridSpec(
        num_scalar_prefetch=1,
        grid=(db, nchunk),
        in_specs=in_specs,
        out_specs=pl.BlockSpec((n_heads, ts, kv_lora), lambda b, j, pt: (0, b, 0)),
        scratch_shapes=[pltpu.VMEM((rows, 1), F32), pltpu.VMEM((rows, 1), F32),
                        pltpu.VMEM((rows, kv_lora), F32),
                        pltpu.VMEM((rows, kv_lora), BF16), pltpu.VMEM((rows, rope_dim), BF16),
                        pltpu.VMEM((page, kv_lora), F32), pltpu.VMEM((page, rope_dim), F32)],
    )
    return pl.pallas_call(
        functools.partial(_sample_attn_kernel, kp=kp, ts=ts, rope_dim=rope_dim),
        grid_spec=grid_spec,
        out_shape=jax.ShapeDtypeStruct((n_heads, n, kv_lora), F32),
        compiler_params=_params(("arbitrary", "arbitrary")),
        name="sample_attention",
    )(pt_flat, qlat, qrope, cnew, krnew, *([cache_c] * kp), *([cache_kr] * kp))


def _post_call(x, o, ple, W, i, j, latent, tm=256):
    n, d = x.shape
    ws = [W['g_post'][i], W['g_ffpre'][i], W['g_ffpost'][i], W['w_o'][j]]
    if latent:
        ws.append(W['w_uv_heads'])
        o_spec = pl.BlockSpec((o.shape[0], tm, o.shape[2]), lambda g: (0, g, 0))
    else:
        o_spec = pl.BlockSpec((tm, o.shape[1]), lambda g: (g, 0))
    ws += [W['w_up'][i], W['w_down'][i], W['ple_gate'][i], W['ple_proj'][i]]
    return pl.pallas_call(
        functools.partial(_post_kernel, latent=latent),
        grid=(n // tm,),
        in_specs=[pl.BlockSpec((tm, d), lambda g: (g, 0)), o_spec,
                  pl.BlockSpec((tm, ple.shape[-1]), lambda g: (g, 0))] + [_wspec(w) for w in ws],
        out_specs=pl.BlockSpec((tm, d), lambda g: (g, 0)),
        out_shape=jax.ShapeDtypeStruct((n, d), F32),
        compiler_params=_params(("parallel",)),
        name=f"attn_out_mlp_{'sample' if latent else 'prompt'}_{j}",
    )(x, o, ple, *ws)


def _rope_tables(pos, rope_dim):
    half = rope_dim // 2
    freqs = 1.0 / (ROPE_THETA ** (jnp.arange(half, dtype=F32) / half))
    ang = pos.astype(F32)[:, None] * freqs[None, :]
    cos, sin = jnp.cos(ang), jnp.sin(ang)
    z = jnp.zeros((pos.shape[0], LANES - rope_dim), F32)
    return (jnp.concatenate([cos, cos, z], axis=-1),
            jnp.concatenate([-sin, sin, z], axis=-1))


def _prep_weights(norm_mix_pre, norm_mix_post, norm_ff_pre, norm_ff_post,
                  conv_pw1, conv_dw, conv_dw_bias, conv_ln_g, conv_ln_b, conv_pw2,
                  kv_in_norm, w_dkv, kv_norm, w_uk, w_uv, w_dq, q_norm, w_uq, w_o,
                  w_up, w_down, ple_proj, ple_gate):
    depth, d = norm_mix_pre.shape
    kv_lora, n_heads, nope = w_uk.shape
    vh = w_uv.shape[-1]
    rope_dim = w_dkv.shape[1] - kv_lora
    n_b = w_dq.shape[0]
    q_lora = w_dq.shape[-1]
    row = lambda g: [g[i][None, :] for i in range(g.shape[0])]
    bf = lambda w: [w[i].astype(BF16) for i in range(w.shape[0])]

    uq = w_uq.reshape(n_b, q_lora, n_heads, nope + rope_dim)
    uq_nope = uq[..., :nope].reshape(n_b, q_lora, n_heads * nope)
    uq_rope = jnp.pad(uq[..., nope:], ((0, 0), (0, 0), (0, 0), (0, LANES - rope_dim)))
    uq_cat = jnp.concatenate([uq_nope, uq_rope.reshape(n_b, q_lora, n_heads * LANES)], axis=-1)

    return {
        'n_heads': n_heads, 'nope': nope, 'rope_dim': rope_dim,
        'sm_scale': float((nope + rope_dim) ** -0.5),
        'g_pre': row(norm_mix_pre), 'g_post': row(norm_mix_post),
        'g_ffpre': row(norm_ff_pre), 'g_ffpost': row(norm_ff_post),
        'pw1': bf(conv_pw1), 'dw': [conv_dw[i] for i in range(conv_dw.shape[0])],
        'dwb': row(conv_dw_bias), 'lng': row(conv_ln_g), 'lnb': row(conv_ln_b),
        'pw2': bf(conv_pw2),
        'g_kvin': kv_in_norm[None, :], 'g_kv': kv_norm[None, :],
        'w_dkv': jnp.pad(w_dkv, ((0, 0), (0, LANES - rope_dim))).astype(BF16),
        'w_uk_flat': w_uk.reshape(kv_lora, n_heads * nope).astype(BF16),
        'w_uv_flat': w_uv.reshape(kv_lora, n_heads * vh).astype(BF16),
        'w_ukt': jnp.transpose(w_uk, (1, 2, 0)).astype(BF16),
        'w_uv_heads': jnp.transpose(w_uv, (1, 0, 2)).astype(BF16),
        'w_dq': bf(w_dq), 'g_q': row(q_norm), 'w_uq': bf(uq_cat), 'w_o': bf(w_o),
        'w_up': bf(w_up), 'w_down': bf(w_down),
        'ple_gate': bf(ple_gate), 'ple_proj': bf(ple_proj),
    }


def kernel(x_prompt, x_sample, state_conv, cache_kv_latent, cache_k_rope, page_table, p_prompt, p_sample, norm_mix_pre, norm_mix_post, norm_ff_pre, norm_ff_post, conv_pw1, conv_dw, conv_dw_bias, conv_ln_g, conv_ln_b, conv_pw2, kv_in_norm, w_dkv, kv_norm, w_uk, w_uv, w_dq, q_norm, w_uq, w_o, w_up, w_down, ple_proj, ple_gate):
    W = _prep_weights(norm_mix_pre, norm_mix_post, norm_ff_pre, norm_ff_post,
                      conv_pw1, conv_dw, conv_dw_bias, conv_ln_g, conv_ln_b, conv_pw2,
                      kv_in_norm, w_dkv, kv_norm, w_uk, w_uv, w_dq, q_norm, w_uq, w_o,
                      w_up, w_down, ple_proj, ple_gate)
    b, t, d = x_prompt.shape
    db, ts, _ = x_sample.shape
    depth = norm_mix_pre.shape[0]
    n_a = conv_pw1.shape[0]
    n_b = depth - n_a
    width = conv_dw.shape[1]
    n_heads, rope_dim = W['n_heads'], W['rope_dim']
    page = cache_kv_latent.shape[1]
    past_len = page_table.shape[1] * page
    tm = 256
    hist_pad = ((0, 0), (CONV_PAD - (width - 1), 0), (0, 0))

    cos_p, sin_p = _rope_tables(jnp.arange(t), rope_dim)
    ple_p = p_prompt.reshape(depth, b * t, -1)
    x = x_prompt
    conv_p = []
    zero_prev = jnp.zeros((b, CONV_PAD, d), F32)
    for i in range(n_a):
        x, st = _conv_layer_prompt(x, ple_p[i], zero_prev, W, i)
        conv_p.append(st[:, CONV_PAD - (width - 1):, :])
    x = x.reshape(b * t, d)
    c_p, kr_p, kcat, v = _kv_call(x, cos_p, sin_p, W, up=True, tm=tm)
    kcat = kcat.reshape(b, t, -1)
    v = v.reshape(b, t, -1)
    for j in range(n_b):
        i = n_a + j
        (qcat,) = _q_call(x, cos_p, sin_p, W, i, j, absorb=False, tm=tm)
        o = _prompt_attn(qcat.reshape(b, t, -1), kcat, v, n_heads)
        x = _post_call(x, o.reshape(b * t, -1), ple_p[i], W, i, j, latent=False, tm=tm)
    y_prompt = x.reshape(b, t, d)

    pos_s = past_len + jnp.arange(ts)
    cos_s, sin_s = _rope_tables(jnp.tile(pos_s, tm // ts), rope_dim)
    ple_s = p_sample.reshape(depth, db * ts, -1)
    x = x_sample.reshape(db * ts, d)
    conv_s = []
    for i in range(n_a):
        x, st = _conv_layer_sample(x, ple_s[i], jnp.pad(state_conv[i], hist_pad), W, i, ts)
        conv_s.append(st[:, CONV_PAD - (width - 1):, :])
    c_s, kr_s = _kv_call(x, cos_s, sin_s, W, up=False, tm=tm)
    for j in range(n_b):
        i = n_a + j
        qlat, qrope = _q_call(x, cos_s, sin_s, W, i, j, absorb=True, tm=tm)
        o_lat = _sample_attn(qlat, qrope, c_s, kr_s, cache_kv_latent, cache_k_rope,
                             page_table, ts)
        x = _post_call(x, o_lat, ple_s[i], W, i, j, latent=True, tm=tm)
    y_sample = x.reshape(db, ts, d)

    return (y_prompt, y_sample, jnp.stack(conv_p), jnp.stack(conv_s),
            c_p.reshape(b, t, -1), kr_p.reshape(b, t, -1),
            c_s.reshape(db, ts, -1), kr_s.reshape(db, ts, -1))
```

```python
import functools

import jax
import jax.numpy as jnp
from jax import lax
from jax.experimental import pallas as pl
from jax.experimental.pallas import tpu as pltpu

F32 = jnp.float32
BF16 = jnp.bfloat16

EPS = 1e-6
ROPE_THETA = 10000.0
LANES = 128
VMEM_LIMIT = 56 * 1024 * 1024

CONV_PAD = 32
FF_CHUNK = 1024


def _rms(x, g):
    return x * lax.rsqrt(jnp.mean(x * x, axis=-1, keepdims=True) + EPS) * g


def _mm(a, w):
    return jnp.dot(a.astype(BF16), w, preferred_element_type=F32)


def _dot_nt(a, b):
    return lax.dot_general(a, b, (((1,), (1,)), ((), ())), preferred_element_type=F32)


def _rope_chunk(chunk, cos_t, sin_t):
    half = chunk.shape[-1] // 4
    lane = lax.broadcasted_iota(jnp.int32, chunk.shape, 1)
    swapped = jnp.where(lane < half,
                        pltpu.roll(chunk, LANES - half, 1),
                        pltpu.roll(chunk, half, 1))
    return chunk * cos_t + swapped * sin_t


def _ffn_ple(x, ple, g_ffpre, g_ffpost, wup, wdown, wgate, wproj):
    d_ff = wup.shape[1]
    h = _rms(x, g_ffpre).astype(BF16)
    acc = jnp.zeros_like(x)
    for c in range(d_ff // FF_CHUNK):
        u = jnp.dot(h, wup[:, c * FF_CHUNK:(c + 1) * FF_CHUNK], preferred_element_type=F32)
        u = jnp.square(jnp.maximum(u, 0.0)).astype(BF16)
        acc = acc + jnp.dot(u, wdown[c * FF_CHUNK:(c + 1) * FF_CHUNK, :],
                            preferred_element_type=F32)
    x = x + _rms(acc, g_ffpost)
    gate = jax.nn.sigmoid(_mm(x, wgate[...]))
    return x + gate * _mm(ple, wproj[...])


def _layer_norm_silu(y, g, b):
    mu = jnp.mean(y, axis=-1, keepdims=True)
    yc = y - mu
    var = jnp.mean(yc * yc, axis=-1, keepdims=True)
    z = yc * lax.rsqrt(var + EPS) * g + b
    return z * jax.nn.sigmoid(z)


def _conv_prompt_kernel(x_ref, ple_ref, prev_ref, gpre, gpost, gffpre, gffpost,
                        pw1, dw, dwb, lng, lnb, pw2, wup, wdown, wgate, wproj,
                        xo_ref, st_ref, buf, ybuf, sh, *, tt, width):
    t = pl.program_id(1)
    d = x_ref.shape[-1]

    @pl.when(t == 0)
    def _():
        buf[0:CONV_PAD, :] = prev_ref[0]

    x = x_ref[0]
    h = _rms(x, gpre[...])
    ag = _mm(h, pw1[...])
    buf[CONV_PAD:CONV_PAD + tt, :] = ag[:, :d] * jax.nn.sigmoid(ag[:, d:])

    off = CONV_PAD - (width - 1)
    n_sh = sh.shape[1]
    for s in range(1, 8):
        sh[s - 1] = buf[s:s + n_sh, :]
    rc, lc = 64, 256
    for r in range(tt // rc):
        for c in range(d // lc):
            cs = slice(c * lc, (c + 1) * lc)
            acc = jnp.broadcast_to(dwb[:, cs], (rc, lc))
            for k in range(width):
                s = (k + off) % 8
                lo = r * rc + (k + off) - s
                tap = buf[lo:lo + rc, cs] if s == 0 else sh[s - 1, lo:lo + rc, cs]
                acc = acc + tap * dw[k:k + 1, cs]
            ybuf[r * rc:(r + 1) * rc, cs] = acc

    y = _layer_norm_silu(ybuf[...], lng[...], lnb[...])
    x = x + _rms(_mm(y, pw2[...]), gpost[...])
    xo_ref[0] = _ffn_ple(x, ple_ref[...], gffpre[...], gffpost[...], wup, wdown, wgate, wproj)

    tail = buf[tt:tt + CONV_PAD, :]
    buf[0:CONV_PAD, :] = tail

    @pl.when(t == pl.num_programs(1) - 1)
    def _():
        st_ref[0] = tail


def _conv_sample_kernel(x_ref, ple_ref, prev_ref, gpre, gpost, gffpre, gffpost,
                        pw1, dw, dwb, lng, lnb, pw2, wup, wdown, wgate, wproj,
                        xo_ref, st_ref, full, ybuf, *, ts, width):
    d = x_ref.shape[-1]
    bb = prev_ref.shape[0]
    x = x_ref[...]
    h = _rms(x, gpre[...])
    ag = _mm(h, pw1[...])
    v = ag[:, :d] * jax.nn.sigmoid(ag[:, d:])
    full[:, 0:CONV_PAD, :] = prev_ref[...]
    full[:, CONV_PAD:CONV_PAD + ts, :] = v.reshape(bb, ts, d)

    off = CONV_PAD - (width - 1)
    sb = 4
    for s in range(bb // sb):
        acc = jnp.broadcast_to(dwb[...].reshape(1, 1, d), (sb, ts, d))
        for k in range(width):
            acc = acc + (full[s * sb:(s + 1) * sb, k + off:k + off + ts, :]
                         * dw[k:k + 1, :].reshape(1, 1, d))
        ybuf[s * sb:(s + 1) * sb, :, :] = acc

    y = _layer_norm_silu(ybuf[...].reshape(bb * ts, d), lng[...], lnb[...])
    x = x + _rms(_mm(y, pw2[...]), gpost[...])
    xo_ref[...] = _ffn_ple(x, ple_ref[...], gffpre[...], gffpost[...], wup, wdown, wgate, wproj)
    st_ref[...] = full[:, ts:ts + CONV_PAD, :]


def _kv_kernel(x_ref, cos_ref, sin_ref, gin, gkv, wdkv, *rest, kv_lora, rope_dim, n_heads, up):
    if up:
        wuk, wuv, c_ref, kr_ref, kcat_ref, v_ref = rest
    else:
        c_ref, kr_ref = rest
    h = _rms(x_ref[...], gin[...])
    ckr = _mm(h, wdkv[...])
    c = _rms(ckr[:, :kv_lora], gkv[...])
    c_ref[...] = c
    kr = _rope_chunk(ckr[:, kv_lora:kv_lora + LANES], cos_ref[...], sin_ref[...])
    kr_ref[...] = kr[:, :rope_dim]
    if up:
        cb = c.astype(BF16)
        kn = jnp.dot(cb, wuk[...], preferred_element_type=F32)
        v_ref[...] = jnp.dot(cb, wuv[...], preferred_element_type=F32).astype(BF16)
        krb = kr.astype(BF16)
        nope = kn.shape[1] // n_heads
        for hh in range(n_heads):
            base = hh * (nope + LANES)
            kcat_ref[:, base:base + nope] = kn[:, hh * nope:(hh + 1) * nope].astype(BF16)
            kcat_ref[:, base + nope:base + nope + LANES] = krb


def _q_kernel(x_ref, cos_ref, sin_ref, gpre, gq, wdq, wuq, *rest, n_heads, nope, scale, absorb):
    h = _rms(x_ref[...], gpre[...])
    cq = _rms(_mm(h, wdq[...]), gq[...])
    q = _mm(cq, wuq[...]) * scale
    cos_t = cos_ref[...]
    sin_t = sin_ref[...]
    if absorb:
        wukt, qlat_ref, qrope_ref = rest
        for hh in range(n_heads):
            qlat_ref[hh] = _mm(q[:, hh * nope:(hh + 1) * nope], wukt[hh])
            rb = n_heads * nope + hh * LANES
            qrope_ref[hh] = _rope_chunk(q[:, rb:rb + LANES], cos_t, sin_t)
    else:
        (qcat_ref,) = rest
        for hh in range(n_heads):
            base = hh * (nope + LANES)
            qcat_ref[:, base:base + nope] = q[:, hh * nope:(hh + 1) * nope].astype(BF16)
            rb = n_heads * nope + hh * LANES
            qcat_ref[:, base + nope:base + nope + LANES] = _rope_chunk(
                q[:, rb:rb + LANES], cos_t, sin_t).astype(BF16)


def _prompt_attn_kernel(q_ref, k_ref, v_ref, o_ref, *, tq):
    t = q_ref.shape[1]
    row = lax.broadcasted_iota(jnp.int32, (tq, tq), 0)
    col = lax.broadcasted_iota(jnp.int32, (tq, tq), 1)
    causal = col <= row
    for qi in range(t // tq):
        lo = qi * tq
        q = q_ref[0, lo:lo + tq, :]
        s_d = jnp.where(causal, _dot_nt(q, k_ref[0, lo:lo + tq, :]), -jnp.inf)
        m = jnp.max(s_d, axis=-1, keepdims=True)
        if qi > 0:
            s_o = _dot_nt(q, k_ref[0, 0:lo, :])
            m = jnp.maximum(m, jnp.max(s_o, axis=-1, keepdims=True))
        p_d = jnp.exp(s_d - m)
        l = jnp.sum(p_d, axis=-1, keepdims=True)
        o = jnp.dot(p_d.astype(BF16), v_ref[0, lo:lo + tq, :], preferred_element_type=F32)
        if qi > 0:
            p_o = jnp.exp(s_o - m)
            l = l + jnp.sum(p_o, axis=-1, keepdims=True)
            o = o + jnp.dot(p_o.astype(BF16), v_ref[0, 0:lo, :], preferred_element_type=F32)
        o_ref[0, lo:lo + tq, :] = (o / l).astype(o_ref.dtype)


def _sample_attn_kernel(pt_ref, qlat_ref, qrope_ref, cnew_ref, krnew_ref, *rest,
                        kp, sub, ts, rope_dim):
    del pt_ref
    c_refs = rest[:kp]
    krt_refs = rest[kp:2 * kp]
    o_ref = rest[2 * kp]
    m_s, l_s, acc_s, qt_s, qrb_s, cbuf, pad_s, kpad_s = rest[2 * kp + 1:]
    j = pl.program_id(1)
    n_heads = qlat_ref.shape[0]
    rows = n_heads * ts
    page = c_refs[0].shape[1]

    @pl.when(j == 0)
    def _():
        m_s[...] = jnp.full(m_s.shape, -jnp.inf, F32)
        l_s[...] = jnp.zeros(l_s.shape, F32)
        acc_s[...] = jnp.zeros(acc_s.shape, F32)
        pad_s[...] = jnp.zeros(pad_s.shape, F32)
        pad_s[0:rows, :] = qlat_ref[...].reshape(rows, qlat_ref.shape[-1])
        qt_s[...] = pad_s[...].T.astype(BF16)
        qrb_s[...] = qrope_ref[...].reshape(rows, LANES)[:, :rope_dim].astype(BF16)

    qrb = qrb_s[...]

    def fold(state, s, cb):
        m_old, l, acc = state
        m_new = jnp.maximum(m_old, jnp.max(s, axis=-1, keepdims=True))
        alpha = jnp.exp(m_old - m_new)
        p = jnp.exp(s - m_new)
        l = alpha * l + jnp.sum(p, axis=-1, keepdims=True)
        acc = alpha * acc + jnp.dot(p.astype(BF16), cb, preferred_element_type=F32)
        return m_new, l, acc

    def scores(cb, rope_scores):
        s_t = jnp.dot(cb, qt_s[...], preferred_element_type=F32)
        return s_t.T[:rows, :] + rope_scores

    state = (m_s[...], l_s[...], acc_s[...])
    for i in range(kp):
        cbuf[i * page:(i + 1) * page, :] = c_refs[i][0].astype(BF16)
    gk = sub * page
    s_parts = []
    for g in range(kp // sub):
        cb = cbuf[g * gk:(g + 1) * gk, :]
        s_rope = jnp.concatenate(
            [jnp.dot(qrb, krt_refs[i][0].astype(BF16), preferred_element_type=F32)
             for i in range(g * sub, (g + 1) * sub)], axis=1)
        s_parts.append(scores(cb, s_rope))
    m_old, l, acc = state
    m_new = m_old
    for s in s_parts:
        m_new = jnp.maximum(m_new, jnp.max(s, axis=-1, keepdims=True))
    alpha = jnp.exp(m_old - m_new)
    l = alpha * l
    acc = alpha * acc
    for g, s in enumerate(s_parts):
        p = jnp.exp(s - m_new)
        l = l + jnp.sum(p, axis=-1, keepdims=True)
        acc = acc + jnp.dot(p.astype(BF16), cbuf[g * gk:(g + 1) * gk, :],
                            preferred_element_type=F32)
    state = (m_new, l, acc)

    @pl.when(j < pl.num_programs(1) - 1)
    def _():
        m_s[...], l_s[...], acc_s[...] = state

    @pl.when(j == pl.num_programs(1) - 1)
    def _():
        pad_s[...] = jnp.zeros(pad_s.shape, F32)
        kpad_s[...] = jnp.zeros(kpad_s.shape, F32)
        pad_s[0:ts, :] = cnew_ref[...]
        kpad_s[0:ts, :] = krnew_ref[...]
        cb = pad_s[...].astype(BF16)
        s = scores(cb, _dot_nt(qrb, kpad_s[...].astype(BF16)))
        t_q = lax.broadcasted_iota(jnp.int32, s.shape, 0) % ts
        t_k = lax.broadcasted_iota(jnp.int32, s.shape, 1)
        _, l, acc = fold(state, jnp.where(t_k <= t_q, s, -jnp.inf), cb)
        o_ref[...] = (acc / l).reshape(o_ref.shape)


def _post_kernel(x_ref, o_ref, ple_ref, gpost, gffpre, gffpost, wo, *rest, latent):
    if latent:
        wuv, wup, wdown, wgate, wproj, xo_ref = rest
        n_heads, _, vh = wuv.shape
        m = jnp.zeros(x_ref.shape, F32)
        for hh in range(n_heads):
            oh = _mm(o_ref[hh], wuv[hh])
            m = m + _mm(oh, wo[hh * vh:(hh + 1) * vh, :])
    else:
        wup, wdown, wgate, wproj, xo_ref = rest
        m = jnp.dot(o_ref[...], wo[...], preferred_element_type=F32)
    x = x_ref[...] + _rms(m, gpost[...])
    xo_ref[...] = _ffn_ple(x, ple_ref[...], gffpre[...], gffpost[...], wup, wdown, wgate, wproj)


def _wspec(arr):
    nd = arr.ndim
    return pl.BlockSpec(arr.shape, lambda *_: (0,) * nd, pipeline_mode=pl.Buffered(1))


def _params(sem):
    return pltpu.CompilerParams(dimension_semantics=sem, vmem_limit_bytes=VMEM_LIMIT)


def _conv_weights(W, i):
    return [W['g_pre'][i], W['g_post'][i], W['g_ffpre'][i], W['g_ffpost'][i],
            W['pw1'][i], W['dw'][i], W['dwb'][i], W['lng'][i], W['lnb'][i], W['pw2'][i],
            W['w_up'][i], W['w_down'][i], W['ple_gate'][i], W['ple_proj'][i]]


def _conv_layer_prompt(x, ple, prev, W, i, tt=256):
    b, t, d = x.shape
    nt = t // tt
    ws = _conv_weights(W, i)
    width = W['dw'][i].shape[0]
    return pl.pallas_call(
        functools.partial(_conv_prompt_kernel, tt=tt, width=width),
        grid=(b, nt),
        in_specs=[pl.BlockSpec((1, tt, d), lambda bi, ti: (bi, ti, 0)),
                  pl.BlockSpec((tt, ple.shape[-1]), lambda bi, ti: (bi * nt + ti, 0)),
                  pl.BlockSpec((1, CONV_PAD, d), lambda bi, ti: (bi, 0, 0))]
                 + [_wspec(w) for w in ws],
        out_specs=[pl.BlockSpec((1, tt, d), lambda bi, ti: (bi, ti, 0)),
                   pl.BlockSpec((1, CONV_PAD, d), lambda bi, ti: (bi, 0, 0))],
        out_shape=[jax.ShapeDtypeStruct((b, t, d), F32),
                   jax.ShapeDtypeStruct((b, CONV_PAD, d), F32)],
        scratch_shapes=[pltpu.VMEM((CONV_PAD + tt, d), F32), pltpu.VMEM((tt, d), F32),
                        pltpu.VMEM((7, tt + CONV_PAD - 8, d), F32)],
        compiler_params=_params(("arbitrary", "arbitrary")),
        name=f"conv_layer_prompt_{i}",
    )(x, ple, prev, *ws)


def _conv_layer_sample(x, ple, prev, W, i, ts, bb=32):
    n, d = x.shape
    tm = bb * ts
    ws = _conv_weights(W, i)
    width = W['dw'][i].shape[0]
    return pl.pallas_call(
        functools.partial(_conv_sample_kernel, ts=ts, width=width),
        grid=(n // tm,),
        in_specs=[pl.BlockSpec((tm, d), lambda g: (g, 0)),
                  pl.BlockSpec((tm, ple.shape[-1]), lambda g: (g, 0)),
                  pl.BlockSpec((bb, CONV_PAD, d), lambda g: (g, 0, 0))]
                 + [_wspec(w) for w in ws],
        out_specs=[pl.BlockSpec((tm, d), lambda g: (g, 0)),
                   pl.BlockSpec((bb, CONV_PAD, d), lambda g: (g, 0, 0))],
        out_shape=[jax.ShapeDtypeStruct((n, d), F32),
                   jax.ShapeDtypeStruct((n // ts, CONV_PAD, d), F32)],
        scratch_shapes=[pltpu.VMEM((bb, CONV_PAD + ts, d), F32), pltpu.VMEM((bb, ts, d), F32)],
        compiler_params=_params(("arbitrary",)),
        name=f"conv_layer_sample_{i}",
    )(x, ple, prev, *ws)


def _table_spec(tab, tm):
    nblk = tab.shape[0] // tm
    return pl.BlockSpec((tm, LANES), lambda g: (g % nblk, 0))


def _kv_call(x, cos_t, sin_t, W, up, tm=256):
    n, d = x.shape
    kv_lora = W['g_kv'].shape[-1]
    rope_dim = W['rope_dim']
    n_heads = W['n_heads']
    ws = [W['g_kvin'], W['g_kv'], W['w_dkv']] + ([W['w_uk_flat'], W['w_uv_flat']] if up else [])
    out_specs = [pl.BlockSpec((tm, kv_lora), lambda g: (g, 0)),
                 pl.BlockSpec((tm, rope_dim), lambda g: (g, 0))]
    out_shape = [jax.ShapeDtypeStruct((n, kv_lora), F32),
                 jax.ShapeDtypeStruct((n, rope_dim), F32)]
    if up:
        kw = W['w_uk_flat'].shape[1] + n_heads * LANES
        vw = W['w_uv_flat'].shape[1]
        out_specs += [pl.BlockSpec((tm, kw), lambda g: (g, 0)),
                      pl.BlockSpec((tm, vw), lambda g: (g, 0))]
        out_shape += [jax.ShapeDtypeStruct((n, kw), BF16), jax.ShapeDtypeStruct((n, vw), BF16)]
    return pl.pallas_call(
        functools.partial(_kv_kernel, kv_lora=kv_lora, rope_dim=rope_dim, n_heads=n_heads, up=up),
        grid=(n // tm,),
        in_specs=[pl.BlockSpec((tm, d), lambda g: (g, 0)),
                  _table_spec(cos_t, tm), _table_spec(sin_t, tm)] + [_wspec(w) for w in ws],
        out_specs=out_specs,
        out_shape=out_shape,
        compiler_params=_params(("parallel",)),
        name="shared_kv_up" if up else "shared_kv",
    )(x, cos_t, sin_t, *ws)


def _q_call(x, cos_t, sin_t, W, i, j, absorb, tm=256):
    n, d = x.shape
    n_heads, nope = W['n_heads'], W['nope']
    ws = [W['g_pre'][i], W['g_q'][j], W['w_dq'][j], W['w_uq'][j]]
    if absorb:
        ws.append(W['w_ukt'])
        kv_lora = W['w_ukt'].shape[-1]
        out_specs = [pl.BlockSpec((n_heads, tm, kv_lora), lambda g: (0, g, 0)),
                     pl.BlockSpec((n_heads, tm, LANES), lambda g: (0, g, 0))]
        out_shape = [jax.ShapeDtypeStruct((n_heads, n, kv_lora), F32),
                     jax.ShapeDtypeStruct((n_heads, n, LANES), F32)]
    else:
        qw = n_heads * (nope + LANES)
        out_specs = [pl.BlockSpec((tm, qw), lambda g: (g, 0))]
        out_shape = [jax.ShapeDtypeStruct((n, qw), BF16)]
    return pl.pallas_call(
        functools.partial(_q_kernel, n_heads=n_heads, nope=nope, scale=W['sm_scale'],
                          absorb=absorb),
        grid=(n // tm,),
        in_specs=[pl.BlockSpec((tm, d), lambda g: (g, 0)),
                  _table_spec(cos_t, tm), _table_spec(sin_t, tm)] + [_wspec(w) for w in ws],
        out_specs=out_specs,
        out_shape=out_shape,
        compiler_params=_params(("parallel",)),
        name=f"mla_query_{'sample' if absorb else 'prompt'}_{j}",
    )(x, cos_t, sin_t, *ws)


def _prompt_attn(qcat, kcat, v, n_heads, tq=256):
    b, t, qw = qcat.shape
    hw = qw // n_heads
    vh = v.shape[-1] // n_heads
    return pl.pallas_call(
        functools.partial(_prompt_attn_kernel, tq=tq),
        grid=(b, n_heads),
        in_specs=[pl.BlockSpec((1, t, hw), lambda bi, hi: (bi, 0, hi)),
                  pl.BlockSpec((1, t, hw), lambda bi, hi: (bi, 0, hi)),
                  pl.BlockSpec((1, t, vh), lambda bi, hi: (bi, 0, hi))],
        out_specs=pl.BlockSpec((1, t, vh), lambda bi, hi: (bi, 0, hi)),
        out_shape=jax.ShapeDtypeStruct((b, t, n_heads * vh), BF16),
        compiler_params=_params(("parallel", "parallel")),
        name="prompt_attention",
    )(qcat, kcat, v)


def _sample_attn(qlat, qrope, cnew, krnew, cache_c, cache_krt, page_table, ts, kp=32, sub=16):
    n_heads, n, kv_lora = qlat.shape
    db, n_pages = page_table.shape
    _, rope_dim, page = cache_krt.shape
    nchunk = n_pages // kp
    rows = n_heads * ts
    pt_flat = page_table.reshape(-1)

    def page_map(i):
        return lambda b, j, pt: (pt[b * n_pages + j * kp + i], 0, 0)

    in_specs = [pl.BlockSpec((n_heads, ts, kv_lora), lambda b, j, pt: (0, b, 0)),
                pl.BlockSpec((n_heads, ts, LANES), lambda b, j, pt: (0, b, 0)),
                pl.BlockSpec((ts, kv_lora), lambda b, j, pt: (b, 0)),
                pl.BlockSpec((ts, rope_dim), lambda b, j, pt: (b, 0))]
    in_specs += [pl.BlockSpec((1, page, kv_lora), page_map(i)) for i in range(kp)]
    in_specs += [pl.BlockSpec((1, rope_dim, page), page_map(i)) for i in range(kp)]
    grid_spec = pltpu.PrefetchScalarGridSpec(
        num_scalar_prefetch=1,
        grid=(db, nchunk),
        in_specs=in_specs,
        out_specs=pl.BlockSpec((n_heads, ts, kv_lora), lambda b, j, pt: (0, b, 0)),
        scratch_shapes=[pltpu.VMEM((rows, 1), F32), pltpu.VMEM((rows, 1), F32),
                        pltpu.VMEM((rows, kv_lora), F32),
                        pltpu.VMEM((kv_lora, LANES), BF16), pltpu.VMEM((rows, rope_dim), BF16),
                        pltpu.VMEM((kp * page, kv_lora), BF16),
                        pltpu.VMEM((LANES, kv_lora), F32), pltpu.VMEM((LANES, rope_dim), F32)],
    )
    return pl.pallas_call(
        functools.partial(_sample_attn_kernel, kp=kp, sub=sub, ts=ts, rope_dim=rope_dim),
        grid_spec=grid_spec,
        out_shape=jax.ShapeDtypeStruct((n_heads, n, kv_lora), F32),
        compiler_params=_params(("arbitrary", "arbitrary")),
        name="sample_attention",
    )(pt_flat, qlat, qrope, cnew, krnew, *([cache_c] * kp), *([cache_krt] * kp))


def _post_call(x, o, ple, W, i, j, latent, tm=256):
    n, d = x.shape
    ws = [W['g_post'][i], W['g_ffpre'][i], W['g_ffpost'][i], W['w_o'][j]]
    if latent:
        ws.append(W['w_uv_heads'])
        o_spec = pl.BlockSpec((o.shape[0], tm, o.shape[2]), lambda g: (0, g, 0))
    else:
        o_spec = pl.BlockSpec((tm, o.shape[1]), lambda g: (g, 0))
    ws += [W['w_up'][i], W['w_down'][i], W['ple_gate'][i], W['ple_proj'][i]]
    return pl.pallas_call(
        functools.partial(_post_kernel, latent=latent),
        grid=(n // tm,),
        in_specs=[pl.BlockSpec((tm, d), lambda g: (g, 0)), o_spec,
                  pl.BlockSpec((tm, ple.shape[-1]), lambda g: (g, 0))] + [_wspec(w) for w in ws],
        out_specs=pl.BlockSpec((tm, d), lambda g: (g, 0)),
        out_shape=jax.ShapeDtypeStruct((n, d), F32),
        compiler_params=_params(("parallel",)),
        name=f"attn_out_mlp_{'sample' if latent else 'prompt'}_{j}",
    )(x, o, ple, *ws)


def _rope_tables(pos, rope_dim):
    half = rope_dim // 2
    freqs = 1.0 / (ROPE_THETA ** (jnp.arange(half, dtype=F32) / half))
    ang = pos.astype(F32)[:, None] * freqs[None, :]
    cos, sin = jnp.cos(ang), jnp.sin(ang)
    z = jnp.zeros((pos.shape[0], LANES - rope_dim), F32)
    return (jnp.concatenate([cos, cos, z], axis=-1),
            jnp.concatenate([-sin, sin, z], axis=-1))


def _prep_weights(norm_mix_pre, norm_mix_post, norm_ff_pre, norm_ff_post,
                  conv_pw1, conv_dw, conv_dw_bias, conv_ln_g, conv_ln_b, conv_pw2,
                  kv_in_norm, w_dkv, kv_norm, w_uk, w_uv, w_dq, q_norm, w_uq, w_o,
                  w_up, w_down, ple_proj, ple_gate):
    depth, d = norm_mix_pre.shape
    kv_lora, n_heads, nope = w_uk.shape
    vh = w_uv.shape[-1]
    rope_dim = w_dkv.shape[1] - kv_lora
    n_b = w_dq.shape[0]
    q_lora = w_dq.shape[-1]
    row = lambda g: [g[i][None, :] for i in range(g.shape[0])]
    bf = lambda w: [w[i].astype(BF16) for i in range(w.shape[0])]

    uq = w_uq.reshape(n_b, q_lora, n_heads, nope + rope_dim)
    uq_nope = uq[..., :nope].reshape(n_b, q_lora, n_heads * nope)
    uq_rope = jnp.pad(uq[..., nope:], ((0, 0), (0, 0), (0, 0), (0, LANES - rope_dim)))
    uq_cat = jnp.concatenate([uq_nope, uq_rope.reshape(n_b, q_lora, n_heads * LANES)], axis=-1)

    return {
        'n_heads': n_heads, 'nope': nope, 'rope_dim': rope_dim,
        'sm_scale': float((nope + rope_dim) ** -0.5),
        'g_pre': row(norm_mix_pre), 'g_post': row(norm_mix_post),
        'g_ffpre': row(norm_ff_pre), 'g_ffpost': row(norm_ff_post),
        'pw1': bf(conv_pw1), 'dw': [conv_dw[i] for i in range(conv_dw.shape[0])],
        'dwb': row(conv_dw_bias), 'lng': row(conv_ln_g), 'lnb': row(conv_ln_b),
        'pw2': bf(conv_pw2),
        'g_kvin': kv_in_norm[None, :], 'g_kv': kv_norm[None, :],
        'w_dkv': jnp.pad(w_dkv, ((0, 0), (0, LANES - rope_dim))).astype(BF16),
        'w_uk_flat': w_uk.reshape(kv_lora, n_heads * nope).astype(BF16),
        'w_uv_flat': w_uv.reshape(kv_lora, n_heads * vh).astype(BF16),
        'w_ukt': jnp.transpose(w_uk, (1, 2, 0)).astype(BF16),
        'w_uv_heads': jnp.transpose(w_uv, (1, 0, 2)).astype(BF16),
        'w_dq': bf(w_dq), 'g_q': row(q_norm), 'w_uq': bf(uq_cat), 'w_o': bf(w_o),
        'w_up': bf(w_up), 'w_down': bf(w_down),
        'ple_gate': bf(ple_gate), 'ple_proj': bf(ple_proj),
    }


def kernel(x_prompt, x_sample, state_conv, cache_kv_latent, cache_k_rope, page_table, p_prompt, p_sample, norm_mix_pre, norm_mix_post, norm_ff_pre, norm_ff_post, conv_pw1, conv_dw, conv_dw_bias, conv_ln_g, conv_ln_b, conv_pw2, kv_in_norm, w_dkv, kv_norm, w_uk, w_uv, w_dq, q_norm, w_uq, w_o, w_up, w_down, ple_proj, ple_gate):
    W = _prep_weights(norm_mix_pre, norm_mix_post, norm_ff_pre, norm_ff_post,
                      conv_pw1, conv_dw, conv_dw_bias, conv_ln_g, conv_ln_b, conv_pw2,
                      kv_in_norm, w_dkv, kv_norm, w_uk, w_uv, w_dq, q_norm, w_uq, w_o,
                      w_up, w_down, ple_proj, ple_gate)
    b, t, d = x_prompt.shape
    db, ts, _ = x_sample.shape
    depth = norm_mix_pre.shape[0]
    n_a = conv_pw1.shape[0]
    n_b = depth - n_a
    width = conv_dw.shape[1]
    n_heads, rope_dim = W['n_heads'], W['rope_dim']
    page = cache_kv_latent.shape[1]
    past_len = page_table.shape[1] * page
    tm, tm_p = 256, 512
    hist_pad = ((0, 0), (CONV_PAD - (width - 1), 0), (0, 0))

    cos_p, sin_p = _rope_tables(jnp.arange(t), rope_dim)
    ple_p = p_prompt.reshape(depth, b * t, -1)
    x = x_prompt
    conv_p = []
    zero_prev = jnp.zeros((b, CONV_PAD, d), F32)
    for i in range(n_a):
        x, st = _conv_layer_prompt(x, ple_p[i], zero_prev, W, i)
        conv_p.append(st[:, CONV_PAD - (width - 1):, :])
    x = x.reshape(b * t, d)
    c_p, kr_p, kcat, v = _kv_call(x, cos_p, sin_p, W, up=True, tm=tm_p)
    kcat = kcat.reshape(b, t, -1)
    v = v.reshape(b, t, -1)
    for j in range(n_b):
        i = n_a + j
        (qcat,) = _q_call(x, cos_p, sin_p, W, i, j, absorb=False, tm=tm_p)
        o = _prompt_attn(qcat.reshape(b, t, -1), kcat, v, n_heads)
        x = _post_call(x, o.reshape(b * t, -1), ple_p[i], W, i, j, latent=False, tm=tm_p)
    y_prompt = x.reshape(b, t, d)

    pos_s = past_len + jnp.arange(ts)
    cos_s, sin_s = _rope_tables(jnp.tile(pos_s, tm // ts), rope_dim)
    ple_s = p_sample.reshape(depth, db * ts, -1)
    x = x_sample.reshape(db * ts, d)
    conv_s = []
    for i in range(n_a):
        x, st = _conv_layer_sample(x, ple_s[i], jnp.pad(state_conv[i], hist_pad), W, i, ts)
        conv_s.append(st[:, CONV_PAD - (width - 1):, :])
    c_s, kr_s = _kv_call(x, cos_s, sin_s, W, up=False, tm=tm)
    cache_krt = jnp.swapaxes(cache_k_rope, 1, 2)
    for j in range(n_b):
        i = n_a + j
        qlat, qrope = _q_call(x, cos_s, sin_s, W, i, j, absorb=True, tm=tm)
        o_lat = _sample_attn(qlat, qrope, c_s, kr_s, cache_kv_latent, cache_krt, page_table, ts)
        x = _post_call(x, o_lat, ple_s[i], W, i, j, latent=True, tm=tm)
    y_sample = x.reshape(db, ts, d)

    return (y_prompt, y_sample, jnp.stack(conv_p), jnp.stack(conv_s),
            c_p.reshape(b, t, -1), kr_p.reshape(b, t, -1),
            c_s.reshape(db, ts, -1), kr_s.reshape(db, ts, -1))
```

```python
import functools

import jax
import jax.numpy as jnp
from jax import lax
from jax.experimental import pallas as pl
from jax.experimental.pallas import tpu as pltpu

F32 = jnp.float32
BF16 = jnp.bfloat16

EPS = 1e-6
ROPE_THETA = 10000.0
LANES = 128
VMEM_LIMIT = 56 * 1024 * 1024

CONV_PAD = 32
FF_CHUNK = 1024


def _rms(x, g):
    return x * lax.rsqrt(jnp.mean(x * x, axis=-1, keepdims=True) + EPS) * g


def _mm(a, w):
    return jnp.dot(a.astype(BF16), w, preferred_element_type=F32)


def _dot_nt(a, b):
    return lax.dot_general(a, b, (((1,), (1,)), ((), ())), preferred_element_type=F32)


def _rope_chunk(chunk, cos_t, sin_t):
    half = chunk.shape[-1] // 4
    lane = lax.broadcasted_iota(jnp.int32, chunk.shape, 1)
    swapped = jnp.where(lane < half,
                        pltpu.roll(chunk, LANES - half, 1),
                        pltpu.roll(chunk, half, 1))
    return chunk * cos_t + swapped * sin_t


def _ffn_ple(x, ple, g_ffpre, g_ffpost, wup, wdown, wgate, wproj):
    d_ff = wup.shape[1]
    h = _rms(x, g_ffpre).astype(BF16)
    acc = jnp.zeros_like(x)
    for c in range(d_ff // FF_CHUNK):
        u = jnp.dot(h, wup[:, c * FF_CHUNK:(c + 1) * FF_CHUNK], preferred_element_type=F32)
        u = jnp.square(jnp.maximum(u, 0.0)).astype(BF16)
        acc = acc + jnp.dot(u, wdown[c * FF_CHUNK:(c + 1) * FF_CHUNK, :],
                            preferred_element_type=F32)
    x = x + _rms(acc, g_ffpost)
    gate = jax.nn.sigmoid(_mm(x, wgate[...]))
    return x + gate * _mm(ple, wproj[...])


def _layer_norm_silu(y, g, b):
    mu = jnp.mean(y, axis=-1, keepdims=True)
    yc = y - mu
    var = jnp.mean(yc * yc, axis=-1, keepdims=True)
    z = yc * lax.rsqrt(var + EPS) * g + b
    return z * jax.nn.sigmoid(z)


def _conv_prompt_kernel(x_ref, ple_ref, prev_ref, gpre, gpost, gffpre, gffpost,
                        pw1, dw, dwb, lng, lnb, pw2, wup, wdown, wgate, wproj,
                        xo_ref, st_ref, buf, ybuf, sh, *, tt, width):
    t = pl.program_id(1)
    d = x_ref.shape[-1]

    @pl.when(t == 0)
    def _():
        buf[0:CONV_PAD, :] = prev_ref[0]

    x = x_ref[0]
    h = _rms(x, gpre[...])
    ag = _mm(h, pw1[...])
    buf[CONV_PAD:CONV_PAD + tt, :] = ag[:, :d] * jax.nn.sigmoid(ag[:, d:])

    off = CONV_PAD - (width - 1)
    n_sh = sh.shape[1]
    for s in range(1, 8):
        sh[s - 1] = buf[s:s + n_sh, :]
    rc, lc = 64, 256
    for r in range(tt // rc):
        for c in range(d // lc):
            cs = slice(c * lc, (c + 1) * lc)
            acc = jnp.broadcast_to(dwb[:, cs], (rc, lc))
            for k in range(width):
                s = (k + off) % 8
                lo = r * rc + (k + off) - s
                tap = buf[lo:lo + rc, cs] if s == 0 else sh[s - 1, lo:lo + rc, cs]
                acc = acc + tap * dw[k:k + 1, cs]
            ybuf[r * rc:(r + 1) * rc, cs] = acc

    y = _layer_norm_silu(ybuf[...], lng[...], lnb[...])
    x = x + _rms(_mm(y, pw2[...]), gpost[...])
    xo_ref[0] = _ffn_ple(x, ple_ref[...], gffpre[...], gffpost[...], wup, wdown, wgate, wproj)

    tail = buf[tt:tt + CONV_PAD, :]
    buf[0:CONV_PAD, :] = tail

    @pl.when(t == pl.num_programs(1) - 1)
    def _():
        st_ref[0] = tail


def _conv_sample_kernel(x_ref, ple_ref, prev_ref, gpre, gpost, gffpre, gffpost,
                        pw1, dw, dwb, lng, lnb, pw2, wup, wdown, wgate, wproj,
                        xo_ref, st_ref, full, ybuf, *, ts, width):
    d = x_ref.shape[-1]
    bb = prev_ref.shape[0]
    x = x_ref[...]
    h = _rms(x, gpre[...])
    ag = _mm(h, pw1[...])
    v = ag[:, :d] * jax.nn.sigmoid(ag[:, d:])
    full[:, 0:CONV_PAD, :] = prev_ref[...]
    full[:, CONV_PAD:CONV_PAD + ts, :] = v.reshape(bb, ts, d)

    off = CONV_PAD - (width - 1)
    sb = 4
    for s in range(bb // sb):
        acc = jnp.broadcast_to(dwb[...].reshape(1, 1, d), (sb, ts, d))
        for k in range(width):
            acc = acc + (full[s * sb:(s + 1) * sb, k + off:k + off + ts, :]
                         * dw[k:k + 1, :].reshape(1, 1, d))
        ybuf[s * sb:(s + 1) * sb, :, :] = acc

    y = _layer_norm_silu(ybuf[...].reshape(bb * ts, d), lng[...], lnb[...])
    x = x + _rms(_mm(y, pw2[...]), gpost[...])
    xo_ref[...] = _ffn_ple(x, ple_ref[...], gffpre[...], gffpost[...], wup, wdown, wgate, wproj)
    st_ref[...] = full[:, ts:ts + CONV_PAD, :]


def _kv_kernel(x_ref, cos_ref, sin_ref, gin, gkv, wdkv, *rest, kv_lora, rope_dim, n_heads, up):
    if up:
        wuk, wuv, c_ref, kr_ref, kcat_ref, v_ref = rest
    else:
        c_ref, kr_ref = rest
    h = _rms(x_ref[...], gin[...])
    ckr = _mm(h, wdkv[...])
    c = _rms(ckr[:, :kv_lora], gkv[...])
    c_ref[...] = c
    kr = _rope_chunk(ckr[:, kv_lora:kv_lora + LANES], cos_ref[...], sin_ref[...])
    kr_ref[...] = kr[:, :rope_dim]
    if up:
        cb = c.astype(BF16)
        kn = jnp.dot(cb, wuk[...], preferred_element_type=F32)
        v_ref[...] = jnp.dot(cb, wuv[...], preferred_element_type=F32).astype(BF16)
        nope = kn.shape[1] // n_heads
        krt = kr.T.astype(BF16)
        for hh in range(n_heads):
            base = hh * (nope + LANES)
            kcat_ref[base:base + nope, :] = kn[:, hh * nope:(hh + 1) * nope].T.astype(BF16)
            kcat_ref[base + nope:base + nope + LANES, :] = krt


def _q_kernel(x_ref, cos_ref, sin_ref, gpre, gq, wdq, wuq, *rest, n_heads, nope, scale, absorb):
    h = _rms(x_ref[...], gpre[...])
    cq = _rms(_mm(h, wdq[...]), gq[...])
    q = _mm(cq, wuq[...]) * scale
    cos_t = cos_ref[...]
    sin_t = sin_ref[...]
    if absorb:
        wukt, qlat_ref, qrope_ref = rest
        for hh in range(n_heads):
            qlat_ref[hh] = _mm(q[:, hh * nope:(hh + 1) * nope], wukt[hh])
            rb = n_heads * nope + hh * LANES
            qrope_ref[hh] = _rope_chunk(q[:, rb:rb + LANES], cos_t, sin_t)
    else:
        (qcat_ref,) = rest
        for hh in range(n_heads):
            base = hh * (nope + LANES)
            qcat_ref[:, base:base + nope] = q[:, hh * nope:(hh + 1) * nope].astype(BF16)
            rb = n_heads * nope + hh * LANES
            qcat_ref[:, base + nope:base + nope + LANES] = _rope_chunk(
                q[:, rb:rb + LANES], cos_t, sin_t).astype(BF16)


def _prompt_attn_kernel(q_ref, k_ref, v_ref, o_ref, *, tq):
    t = q_ref.shape[1]
    row = lax.broadcasted_iota(jnp.int32, (tq, tq), 0)
    col = lax.broadcasted_iota(jnp.int32, (tq, tq), 1)
    causal = col <= row
    for qi in range(t // tq):
        lo = qi * tq
        q = q_ref[0, lo:lo + tq, :]
        s_d = jnp.where(causal, jnp.dot(q, k_ref[:, lo:lo + tq], preferred_element_type=F32),
                        -jnp.inf)
        m = jnp.max(s_d, axis=-1, keepdims=True)
        if qi > 0:
            s_o = jnp.dot(q, k_ref[:, 0:lo], preferred_element_type=F32)
            m = jnp.maximum(m, jnp.max(s_o, axis=-1, keepdims=True))
        p_d = jnp.exp(s_d - m)
        l = jnp.sum(p_d, axis=-1, keepdims=True)
        o = jnp.dot(p_d.astype(BF16), v_ref[0, lo:lo + tq, :], preferred_element_type=F32)
        if qi > 0:
            p_o = jnp.exp(s_o - m)
            l = l + jnp.sum(p_o, axis=-1, keepdims=True)
            o = o + jnp.dot(p_o.astype(BF16), v_ref[0, 0:lo, :], preferred_element_type=F32)
        o_ref[0, lo:lo + tq, :] = (o / l).astype(o_ref.dtype)


def _sample_attn_kernel(pt_ref, qlat_ref, qrope_ref, cnew_ref, krnew_ref, cache_c, cache_krt,
                        o_ref, m_s, l_s, acc_s, qt_s, qrb_s, cbuf, pad_s, kpad_s, cf, kf, sem,
                        *, kp, sub, ts, rope_dim):
    j = pl.program_id(1)
    nj = pl.num_programs(1)
    step = pl.program_id(0) * nj + j
    n_steps = pl.num_programs(0) * nj
    slot = step % 2
    n_heads = qlat_ref.shape[0]
    rows = n_heads * ts
    page = cf.shape[2]

    def page_copies(step_idx, slot_idx):
        copies = []
        for i in range(kp):
            pid = pt_ref[step_idx * kp + i]
            copies.append(pltpu.make_async_copy(cache_c.at[pid], cf.at[slot_idx, i],
                                                sem.at[slot_idx]))
            copies.append(pltpu.make_async_copy(cache_krt.at[pid], kf.at[slot_idx, i],
                                                sem.at[slot_idx]))
        return copies

    @pl.when(step == 0)
    def _():
        for cp in page_copies(step, slot):
            cp.start()

    @pl.when(step + 1 < n_steps)
    def _():
        for cp in page_copies(step + 1, 1 - slot):
            cp.start()

    @pl.when(j == 0)
    def _():
        m_s[...] = jnp.full(m_s.shape, -jnp.inf, F32)
        l_s[...] = jnp.zeros(l_s.shape, F32)
        acc_s[...] = jnp.zeros(acc_s.shape, F32)
        pad_s[...] = jnp.zeros(pad_s.shape, F32)
        pad_s[0:rows, :] = qlat_ref[...].reshape(rows, qlat_ref.shape[-1])
        qt_s[...] = pad_s[...].T.astype(BF16)
        qrb_s[...] = qrope_ref[...].reshape(rows, LANES)[:, :rope_dim].astype(BF16)

    for cp in page_copies(step, slot):
        cp.wait()

    qrb = qrb_s[...]

    def fold(state, s, cb):
        m_old, l, acc = state
        m_new = jnp.maximum(m_old, jnp.max(s, axis=-1, keepdims=True))
        alpha = jnp.exp(m_old - m_new)
        p = jnp.exp(s - m_new)
        l = alpha * l + jnp.sum(p, axis=-1, keepdims=True)
        acc = alpha * acc + jnp.dot(p.astype(BF16), cb, preferred_element_type=F32)
        return m_new, l, acc

    def scores(cb, rope_scores):
        s_t = jnp.dot(cb, qt_s[...], preferred_element_type=F32)
        return s_t.T[:rows, :] + rope_scores

    state = (m_s[...], l_s[...], acc_s[...])
    for i in range(kp):
        cbuf[i * page:(i + 1) * page, :] = cf[slot, i].astype(BF16)
    gk = sub * page
    s_parts = []
    for g in range(kp // sub):
        cb = cbuf[g * gk:(g + 1) * gk, :]
        s_rope = jnp.concatenate(
            [jnp.dot(qrb, kf[slot, i].astype(BF16), preferred_element_type=F32)
             for i in range(g * sub, (g + 1) * sub)], axis=1)
        s_parts.append(scores(cb, s_rope))
    m_old, l, acc = state
    m_new = m_old
    for s in s_parts:
        m_new = jnp.maximum(m_new, jnp.max(s, axis=-1, keepdims=True))
    alpha = jnp.exp(m_old - m_new)
    l = alpha * l
    acc = alpha * acc
    for g, s in enumerate(s_parts):
        p = jnp.exp(s - m_new)
        l = l + jnp.sum(p, axis=-1, keepdims=True)
        acc = acc + jnp.dot(p.astype(BF16), cbuf[g * gk:(g + 1) * gk, :],
                            preferred_element_type=F32)
    state = (m_new, l, acc)

    @pl.when(j < nj - 1)
    def _():
        m_s[...], l_s[...], acc_s[...] = state

    @pl.when(j == nj - 1)
    def _():
        pad_s[...] = jnp.zeros(pad_s.shape, F32)
        kpad_s[...] = jnp.zeros(kpad_s.shape, F32)
        pad_s[0:ts, :] = cnew_ref[...]
        kpad_s[0:ts, :] = krnew_ref[...]
        cb = pad_s[...].astype(BF16)
        s = scores(cb, _dot_nt(qrb, kpad_s[...].astype(BF16)))
        t_q = lax.broadcasted_iota(jnp.int32, s.shape, 0) % ts
        t_k = lax.broadcasted_iota(jnp.int32, s.shape, 1)
        _, l, acc = fold(state, jnp.where(t_k <= t_q, s, -jnp.inf), cb)
        o_ref[...] = (acc / l).reshape(o_ref.shape)


def _post_kernel(x_ref, o_ref, ple_ref, gpost, gffpre, gffpost, wo, *rest, latent):
    if latent:
        wuv, wup, wdown, wgate, wproj, xo_ref = rest
        n_heads, _, vh = wuv.shape
        m = jnp.zeros(x_ref.shape, F32)
        for hh in range(n_heads):
            oh = _mm(o_ref[hh], wuv[hh])
            m = m + _mm(oh, wo[hh * vh:(hh + 1) * vh, :])
    else:
        wup, wdown, wgate, wproj, xo_ref = rest
        m = jnp.dot(o_ref[...], wo[...], preferred_element_type=F32)
    x = x_ref[...] + _rms(m, gpost[...])
    xo_ref[...] = _ffn_ple(x, ple_ref[...], gffpre[...], gffpost[...], wup, wdown, wgate, wproj)


def _wspec(arr):
    nd = arr.ndim
    return pl.BlockSpec(arr.shape, lambda *_: (0,) * nd, pipeline_mode=pl.Buffered(1))


def _params(sem):
    return pltpu.CompilerParams(dimension_semantics=sem, vmem_limit_bytes=VMEM_LIMIT)


def _conv_weights(W, i):
    return [W['g_pre'][i], W['g_post'][i], W['g_ffpre'][i], W['g_ffpost'][i],
            W['pw1'][i], W['dw'][i], W['dwb'][i], W['lng'][i], W['lnb'][i], W['pw2'][i],
            W['w_up'][i], W['w_down'][i], W['ple_gate'][i], W['ple_proj'][i]]


def _conv_layer_prompt(x, ple, prev, W, i, tt=256):
    b, t, d = x.shape
    nt = t // tt
    ws = _conv_weights(W, i)
    width = W['dw'][i].shape[0]
    return pl.pallas_call(
        functools.partial(_conv_prompt_kernel, tt=tt, width=width),
        grid=(b, nt),
        in_specs=[pl.BlockSpec((1, tt, d), lambda bi, ti: (bi, ti, 0)),
                  pl.BlockSpec((None, tt, ple.shape[-1]), lambda bi, ti: (i, bi * nt + ti, 0)),
                  pl.BlockSpec((1, CONV_PAD, d), lambda bi, ti: (bi, 0, 0))]
                 + [_wspec(w) for w in ws],
        out_specs=[pl.BlockSpec((1, tt, d), lambda bi, ti: (bi, ti, 0)),
                   pl.BlockSpec((1, CONV_PAD, d), lambda bi, ti: (bi, 0, 0))],
        out_shape=[jax.ShapeDtypeStruct((b, t, d), F32),
                   jax.ShapeDtypeStruct((b, CONV_PAD, d), F32)],
        scratch_shapes=[pltpu.VMEM((CONV_PAD + tt, d), F32), pltpu.VMEM((tt, d), F32),
                        pltpu.VMEM((7, tt + CONV_PAD - 8, d), F32)],
        compiler_params=_params(("arbitrary", "arbitrary")),
        name=f"conv_layer_prompt_{i}",
    )(x, ple, prev, *ws)


def _conv_layer_sample(x, ple, prev, W, i, ts, bb=32):
    n, d = x.shape
    tm = bb * ts
    ws = _conv_weights(W, i)
    width = W['dw'][i].shape[0]
    return pl.pallas_call(
        functools.partial(_conv_sample_kernel, ts=ts, width=width),
        grid=(n // tm,),
        in_specs=[pl.BlockSpec((tm, d), lambda g: (g, 0)),
                  pl.BlockSpec((None, tm, ple.shape[-1]), lambda g: (i, g, 0)),
                  pl.BlockSpec((bb, CONV_PAD, d), lambda g: (g, 0, 0))]
                 + [_wspec(w) for w in ws],
        out_specs=[pl.BlockSpec((tm, d), lambda g: (g, 0)),
                   pl.BlockSpec((bb, CONV_PAD, d), lambda g: (g, 0, 0))],
        out_shape=[jax.ShapeDtypeStruct((n, d), F32),
                   jax.ShapeDtypeStruct((n // ts, CONV_PAD, d), F32)],
        scratch_shapes=[pltpu.VMEM((bb, CONV_PAD + ts, d), F32), pltpu.VMEM((bb, ts, d), F32)],
        compiler_params=_params(("arbitrary",)),
        name=f"conv_layer_sample_{i}",
    )(x, ple, prev, *ws)


def _table_spec(tab, tm):
    nblk = tab.shape[0] // tm
    return pl.BlockSpec((tm, LANES), lambda g: (g % nblk, 0))


def _kv_call(x, cos_t, sin_t, W, up, tm=256):
    n, d = x.shape
    kv_lora = W['g_kv'].shape[-1]
    rope_dim = W['rope_dim']
    n_heads = W['n_heads']
    ws = [W['g_kvin'], W['g_kv'], W['w_dkv']] + ([W['w_uk_flat'], W['w_uv_flat']] if up else [])
    out_specs = [pl.BlockSpec((tm, kv_lora), lambda g: (g, 0)),
                 pl.BlockSpec((tm, rope_dim), lambda g: (g, 0))]
    out_shape = [jax.ShapeDtypeStruct((n, kv_lora), F32),
                 jax.ShapeDtypeStruct((n, rope_dim), F32)]
    if up:
        kw = W['w_uk_flat'].shape[1] + n_heads * LANES
        vw = W['w_uv_flat'].shape[1]
        out_specs += [pl.BlockSpec((kw, tm), lambda g: (0, g)),
                      pl.BlockSpec((tm, vw), lambda g: (g, 0))]
        out_shape += [jax.ShapeDtypeStruct((kw, n), BF16), jax.ShapeDtypeStruct((n, vw), BF16)]
    return pl.pallas_call(
        functools.partial(_kv_kernel, kv_lora=kv_lora, rope_dim=rope_dim, n_heads=n_heads, up=up),
        grid=(n // tm,),
        in_specs=[pl.BlockSpec((tm, d), lambda g: (g, 0)),
                  _table_spec(cos_t, tm), _table_spec(sin_t, tm)] + [_wspec(w) for w in ws],
        out_specs=out_specs,
        out_shape=out_shape,
        compiler_params=_params(("parallel",)),
        name="shared_kv_up" if up else "shared_kv",
    )(x, cos_t, sin_t, *ws)


def _q_call(x, cos_t, sin_t, W, i, j, absorb, tm=256):
    n, d = x.shape
    n_heads, nope = W['n_heads'], W['nope']
    ws = [W['g_pre'][i], W['g_q'][j], W['w_dq'][j], W['w_uq'][j]]
    if absorb:
        ws.append(W['w_ukt'])
        kv_lora = W['w_ukt'].shape[-1]
        out_specs = [pl.BlockSpec((n_heads, tm, kv_lora), lambda g: (0, g, 0)),
                     pl.BlockSpec((n_heads, tm, LANES), lambda g: (0, g, 0))]
        out_shape = [jax.ShapeDtypeStruct((n_heads, n, kv_lora), F32),
                     jax.ShapeDtypeStruct((n_heads, n, LANES), F32)]
    else:
        qw = n_heads * (nope + LANES)
        out_specs = [pl.BlockSpec((tm, qw), lambda g: (g, 0))]
        out_shape = [jax.ShapeDtypeStruct((n, qw), BF16)]
    return pl.pallas_call(
        functools.partial(_q_kernel, n_heads=n_heads, nope=nope, scale=W['sm_scale'],
                          absorb=absorb),
        grid=(n // tm,),
        in_specs=[pl.BlockSpec((tm, d), lambda g: (g, 0)),
                  _table_spec(cos_t, tm), _table_spec(sin_t, tm)] + [_wspec(w) for w in ws],
        out_specs=out_specs,
        out_shape=out_shape,
        compiler_params=_params(("parallel",)),
        name=f"mla_query_{'sample' if absorb else 'prompt'}_{j}",
    )(x, cos_t, sin_t, *ws)


def _prompt_attn(qcat, kcat_t, v, n_heads, tq=256):
    b, t, qw = qcat.shape
    hw = qw // n_heads
    vh = v.shape[-1] // n_heads
    return pl.pallas_call(
        functools.partial(_prompt_attn_kernel, tq=tq),
        grid=(b, n_heads),
        in_specs=[pl.BlockSpec((1, t, hw), lambda bi, hi: (bi, 0, hi)),
                  pl.BlockSpec((hw, t), lambda bi, hi: (hi, bi)),
                  pl.BlockSpec((1, t, vh), lambda bi, hi: (bi, 0, hi))],
        out_specs=pl.BlockSpec((1, t, vh), lambda bi, hi: (bi, 0, hi)),
        out_shape=jax.ShapeDtypeStruct((b, t, n_heads * vh), BF16),
        compiler_params=_params(("parallel", "parallel")),
        name="prompt_attention",
    )(qcat, kcat_t, v)


def _sample_attn(qlat, qrope, cnew, krnew, cache_c, cache_krt, page_table, ts, kp=32, sub=16):
    n_heads, n, kv_lora = qlat.shape
    db, n_pages = page_table.shape
    _, rope_dim, page = cache_krt.shape
    nchunk = n_pages // kp
    rows = n_heads * ts
    pt_flat = page_table.reshape(-1)

    in_specs = [pl.BlockSpec((n_heads, ts, kv_lora), lambda b, j, pt: (0, b, 0)),
                pl.BlockSpec((n_heads, ts, LANES), lambda b, j, pt: (0, b, 0)),
                pl.BlockSpec((ts, kv_lora), lambda b, j, pt: (b, 0)),
                pl.BlockSpec((ts, rope_dim), lambda b, j, pt: (b, 0)),
                pl.BlockSpec(memory_space=pl.ANY), pl.BlockSpec(memory_space=pl.ANY)]
    grid_spec = pltpu.PrefetchScalarGridSpec(
        num_scalar_prefetch=1,
        grid=(db, nchunk),
        in_specs=in_specs,
        out_specs=pl.BlockSpec((n_heads, ts, kv_lora), lambda b, j, pt: (0, b, 0)),
        scratch_shapes=[pltpu.VMEM((rows, 1), F32), pltpu.VMEM((rows, 1), F32),
                        pltpu.VMEM((rows, kv_lora), F32),
                        pltpu.VMEM((kv_lora, LANES), BF16), pltpu.VMEM((rows, rope_dim), BF16),
                        pltpu.VMEM((kp * page, kv_lora), BF16),
                        pltpu.VMEM((LANES, kv_lora), F32), pltpu.VMEM((LANES, rope_dim), F32),
                        pltpu.VMEM((2, kp, page, kv_lora), F32),
                        pltpu.VMEM((2, kp, rope_dim, page), F32),
                        pltpu.SemaphoreType.DMA((2,))],
    )
    return pl.pallas_call(
        functools.partial(_sample_attn_kernel, kp=kp, sub=sub, ts=ts, rope_dim=rope_dim),
        grid_spec=grid_spec,
        out_shape=jax.ShapeDtypeStruct((n_heads, n, kv_lora), F32),
        compiler_params=_params(("arbitrary", "arbitrary")),
        name="sample_attention",
    )(pt_flat, qlat, qrope, cnew, krnew, cache_c, cache_krt)


def _post_call(x, o, ple, W, i, j, latent, tm=256):
    n, d = x.shape
    ws = [W['g_post'][i], W['g_ffpre'][i], W['g_ffpost'][i], W['w_o'][j]]
    if latent:
        ws.append(W['w_uv_heads'])
        o_spec = pl.BlockSpec((o.shape[0], tm, o.shape[2]), lambda g: (0, g, 0))
    else:
        o_spec = pl.BlockSpec((tm, o.shape[1]), lambda g: (g, 0))
    ws += [W['w_up'][i], W['w_down'][i], W['ple_gate'][i], W['ple_proj'][i]]
    return pl.pallas_call(
        functools.partial(_post_kernel, latent=latent),
        grid=(n // tm,),
        in_specs=[pl.BlockSpec((tm, d), lambda g: (g, 0)), o_spec,
                  pl.BlockSpec((None, tm, ple.shape[-1]), lambda g: (i, g, 0))]
                 + [_wspec(w) for w in ws],
        out_specs=pl.BlockSpec((tm, d), lambda g: (g, 0)),
        out_shape=jax.ShapeDtypeStruct((n, d), F32),
        compiler_params=_params(("parallel",)),
        name=f"attn_out_mlp_{'sample' if latent else 'prompt'}_{j}",
    )(x, o, ple, *ws)


def _rope_tables(pos, rope_dim):
    half = rope_dim // 2
    freqs = 1.0 / (ROPE_THETA ** (jnp.arange(half, dtype=F32) / half))
    ang = pos.astype(F32)[:, None] * freqs[None, :]
    cos, sin = jnp.cos(ang), jnp.sin(ang)
    z = jnp.zeros((pos.shape[0], LANES - rope_dim), F32)
    return (jnp.concatenate([cos, cos, z], axis=-1),
            jnp.concatenate([-sin, sin, z], axis=-1))


def _prep_weights(norm_mix_pre, norm_mix_post, norm_ff_pre, norm_ff_post,
                  conv_pw1, conv_dw, conv_dw_bias, conv_ln_g, conv_ln_b, conv_pw2,
                  kv_in_norm, w_dkv, kv_norm, w_uk, w_uv, w_dq, q_norm, w_uq, w_o,
                  w_up, w_down, ple_proj, ple_gate):
    depth, d = norm_mix_pre.shape
    kv_lora, n_heads, nope = w_uk.shape
    vh = w_uv.shape[-1]
    rope_dim = w_dkv.shape[1] - kv_lora
    n_b = w_dq.shape[0]
    q_lora = w_dq.shape[-1]
    row = lambda g: [g[i][None, :] for i in range(g.shape[0])]
    bf = lambda w: [w[i].astype(BF16) for i in range(w.shape[0])]

    uq = w_uq.reshape(n_b, q_lora, n_heads, nope + rope_dim)
    uq_nope = uq[..., :nope].reshape(n_b, q_lora, n_heads * nope)
    uq_rope = jnp.pad(uq[..., nope:], ((0, 0), (0, 0), (0, 0), (0, LANES - rope_dim)))
    uq_cat = jnp.concatenate([uq_nope, uq_rope.reshape(n_b, q_lora, n_heads * LANES)], axis=-1)

    return {
        'n_heads': n_heads, 'nope': nope, 'rope_dim': rope_dim,
        'sm_scale': float((nope + rope_dim) ** -0.5),
        'g_pre': row(norm_mix_pre), 'g_post': row(norm_mix_post),
        'g_ffpre': row(norm_ff_pre), 'g_ffpost': row(norm_ff_post),
        'pw1': bf(conv_pw1), 'dw': [conv_dw[i] for i in range(conv_dw.shape[0])],
        'dwb': row(conv_dw_bias), 'lng': row(conv_ln_g), 'lnb': row(conv_ln_b),
        'pw2': bf(conv_pw2),
        'g_kvin': kv_in_norm[None, :], 'g_kv': kv_norm[None, :],
        'w_dkv': jnp.pad(w_dkv, ((0, 0), (0, LANES - rope_dim))).astype(BF16),
        'w_uk_flat': w_uk.reshape(kv_lora, n_heads * nope).astype(BF16),
        'w_uv_flat': w_uv.reshape(kv_lora, n_heads * vh).astype(BF16),
        'w_ukt': jnp.transpose(w_uk, (1, 2, 0)).astype(BF16),
        'w_uv_heads': jnp.transpose(w_uv, (1, 0, 2)).astype(BF16),
        'w_dq': bf(w_dq), 'g_q': row(q_norm), 'w_uq': bf(uq_cat), 'w_o': bf(w_o),
        'w_up': bf(w_up), 'w_down': bf(w_down),
        'ple_gate': bf(ple_gate), 'ple_proj': bf(ple_proj),
    }


def kernel(x_prompt, x_sample, state_conv, cache_kv_latent, cache_k_rope, page_table, p_prompt, p_sample, norm_mix_pre, norm_mix_post, norm_ff_pre, norm_ff_post, conv_pw1, conv_dw, conv_dw_bias, conv_ln_g, conv_ln_b, conv_pw2, kv_in_norm, w_dkv, kv_norm, w_uk, w_uv, w_dq, q_norm, w_uq, w_o, w_up, w_down, ple_proj, ple_gate):
    W = _prep_weights(norm_mix_pre, norm_mix_post, norm_ff_pre, norm_ff_post,
                      conv_pw1, conv_dw, conv_dw_bias, conv_ln_g, conv_ln_b, conv_pw2,
                      kv_in_norm, w_dkv, kv_norm, w_uk, w_uv, w_dq, q_norm, w_uq, w_o,
                      w_up, w_down, ple_proj, ple_gate)
    b, t, d = x_prompt.shape
    db, ts, _ = x_sample.shape
    depth = norm_mix_pre.shape[0]
    n_a = conv_pw1.shape[0]
    n_b = depth - n_a
    width = conv_dw.shape[1]
    n_heads, rope_dim = W['n_heads'], W['rope_dim']
    page = cache_kv_latent.shape[1]
    past_len = page_table.shape[1] * page
    tm, tm_p = 256, 512
    hist_pad = ((0, 0), (CONV_PAD - (width - 1), 0), (0, 0))

    cos_p, sin_p = _rope_tables(jnp.arange(t), rope_dim)
    ple_p = p_prompt.reshape(depth, b * t, -1)
    x = x_prompt
    conv_p = []
    zero_prev = jnp.zeros((b, CONV_PAD, d), F32)
    for i in range(n_a):
        x, st = _conv_layer_prompt(x, ple_p, zero_prev, W, i)
        conv_p.append(st[:, CONV_PAD - (width - 1):, :])
    x = x.reshape(b * t, d)
    c_p, kr_p, kcat, v = _kv_call(x, cos_p, sin_p, W, up=True, tm=tm_p)
    v = v.reshape(b, t, -1)
    for j in range(n_b):
        i = n_a + j
        (qcat,) = _q_call(x, cos_p, sin_p, W, i, j, absorb=False, tm=tm_p)
        o = _prompt_attn(qcat.reshape(b, t, -1), kcat, v, n_heads)
        x = _post_call(x, o.reshape(b * t, -1), ple_p, W, i, j, latent=False, tm=tm_p)
    y_prompt = x.reshape(b, t, d)

    pos_s = past_len + jnp.arange(ts)
    cos_s, sin_s = _rope_tables(jnp.tile(pos_s, tm // ts), rope_dim)
    ple_s = p_sample.reshape(depth, db * ts, -1)
    x = x_sample.reshape(db * ts, d)
    conv_s = []
    for i in range(n_a):
        x, st = _conv_layer_sample(x, ple_s, jnp.pad(state_conv[i], hist_pad), W, i, ts)
        conv_s.append(st[:, CONV_PAD - (width - 1):, :])
    c_s, kr_s = _kv_call(x, cos_s, sin_s, W, up=False, tm=tm)
    cache_krt = jnp.swapaxes(cache_k_rope, 1, 2)
    for j in range(n_b):
        i = n_a + j
        qlat, qrope = _q_call(x, cos_s, sin_s, W, i, j, absorb=True, tm=tm)
        o_lat = _sample_attn(qlat, qrope, c_s, kr_s, cache_kv_latent, cache_krt, page_table, ts)
        x = _post_call(x, o_lat, ple_s, W, i, j, latent=True, tm=tm)
    y_sample = x.reshape(db, ts, d)

    return (y_prompt, y_sample, jnp.stack(conv_p), jnp.stack(conv_s),
            c_p.reshape(b, t, -1), kr_p.reshape(b, t, -1),
            c_s.reshape(db, ts, -1), kr_s.reshape(db, ts, -1))
```

```python
import functools

import jax
import jax.numpy as jnp
from jax import lax
from jax.experimental import pallas as pl
from jax.experimental.pallas import tpu as pltpu

F32 = jnp.float32
BF16 = jnp.bfloat16

EPS = 1e-6
ROPE_THETA = 10000.0
LANES = 128
VMEM_LIMIT = 56 * 1024 * 1024

CONV_PAD = 32
FF_CHUNK = 1024


def _rms(x, g):
    return x * lax.rsqrt(jnp.mean(x * x, axis=-1, keepdims=True) + EPS) * g


def _mm(a, w):
    return jnp.dot(a.astype(BF16), w, preferred_element_type=F32)


def _dot_nt(a, b):
    return lax.dot_general(a, b, (((1,), (1,)), ((), ())), preferred_element_type=F32)


def _rope_chunk(chunk, cos_t, sin_t):
    half = chunk.shape[-1] // 4
    lane = lax.broadcasted_iota(jnp.int32, chunk.shape, 1)
    swapped = jnp.where(lane < half,
                        pltpu.roll(chunk, LANES - half, 1),
                        pltpu.roll(chunk, half, 1))
    return chunk * cos_t + swapped * sin_t


def _ffn_ple(x, ple, g_ffpre, g_ffpost, wup, wdown, wgate, wproj):
    d_ff = wup.shape[1]
    h = _rms(x, g_ffpre).astype(BF16)
    acc = jnp.zeros_like(x)
    for c in range(d_ff // FF_CHUNK):
        u = jnp.dot(h, wup[:, c * FF_CHUNK:(c + 1) * FF_CHUNK], preferred_element_type=F32)
        u = jnp.square(jnp.maximum(u, 0.0)).astype(BF16)
        acc = acc + jnp.dot(u, wdown[c * FF_CHUNK:(c + 1) * FF_CHUNK, :],
                            preferred_element_type=F32)
    x = x + _rms(acc, g_ffpost)
    gate = jax.nn.sigmoid(_mm(x, wgate[...]))
    return x + gate * _mm(ple, wproj[...])


def _layer_norm_silu(y, g, b):
    mu = jnp.mean(y, axis=-1, keepdims=True)
    yc = y - mu
    var = jnp.mean(yc * yc, axis=-1, keepdims=True)
    z = yc * lax.rsqrt(var + EPS) * g + b
    return z * jax.nn.sigmoid(z)


def _conv_prompt_kernel(x_ref, ple_ref, prev_ref, gpre, gpost, gffpre, gffpost,
                        pw1, dw, dwb, lng, lnb, pw2, wup, wdown, wgate, wproj,
                        xo_ref, st_ref, buf, ybuf, sh, xm_prev, *, tt, nt, width):
    g = pl.program_id(0)
    n_tiles = pl.num_programs(0) - 1
    t = jnp.minimum(g, n_tiles - 1) % nt
    d = x_ref.shape[-1]

    @pl.when(g == 0)
    def _():
        xm_prev[...] = jnp.zeros(xm_prev.shape, F32)

    @pl.when(t == 0)
    def _():
        buf[0:CONV_PAD, :] = prev_ref[0]

    x = x_ref[0]
    h = _rms(x, gpre[...])
    ag = _mm(h, pw1[...])
    buf[CONV_PAD:CONV_PAD + tt, :] = ag[:, :d] * jax.nn.sigmoid(ag[:, d:])
    off = CONV_PAD - (width - 1)
    n_sh = sh.shape[1]
    for s in range(1, 8):
        sh[s - 1] = buf[s:s + n_sh, :]
    rc, lc = 64, 256

    def conv_rows(r):
        for c in range(d // lc):
            cs = slice(c * lc, (c + 1) * lc)
            acc = jnp.broadcast_to(dwb[:, cs], (rc, lc))
            for k in range(width):
                s = (k + off) % 8
                lo = r * rc + (k + off) - s
                tap = buf[lo:lo + rc, cs] if s == 0 else sh[s - 1, lo:lo + rc, cs]
                acc = acc + tap * dw[k:k + 1, cs]
            ybuf[r * rc:(r + 1) * rc, cs] = acc

    xp = xm_prev[...]
    hp = _rms(xp, gffpre[...]).astype(BF16)
    n_ff = wup.shape[1] // FF_CHUNK
    n_rb = tt // rc
    acc_ff = jnp.zeros_like(xp)
    for c in range(n_ff):
        u = jnp.dot(hp, wup[:, c * FF_CHUNK:(c + 1) * FF_CHUNK], preferred_element_type=F32)
        u = jnp.square(jnp.maximum(u, 0.0)).astype(BF16)
        acc_ff = acc_ff + jnp.dot(u, wdown[c * FF_CHUNK:(c + 1) * FF_CHUNK, :],
                                  preferred_element_type=F32)
        for r in range(c * n_rb // n_ff, (c + 1) * n_rb // n_ff):
            conv_rows(r)
    xp = xp + _rms(acc_ff, gffpost[...])
    gate = jax.nn.sigmoid(_mm(xp, wgate[...]))
    xo_ref[0] = xp + gate * _mm(ple_ref[...], wproj[...])

    y = _layer_norm_silu(ybuf[...], lng[...], lnb[...])
    xm_prev[...] = x + _rms(_mm(y, pw2[...]), gpost[...])

    tail = buf[tt:tt + CONV_PAD, :]
    buf[0:CONV_PAD, :] = tail

    @pl.when(t == nt - 1)
    def _():
        st_ref[0] = tail


def _conv_sample_kernel(x_ref, ple_ref, prev_ref, gpre, gpost, gffpre, gffpost,
                        pw1, dw, dwb, lng, lnb, pw2, wup, wdown, wgate, wproj,
                        xo_ref, st_ref, full, ybuf, *, ts, width):
    d = x_ref.shape[-1]
    bb = prev_ref.shape[0]
    x = x_ref[...]
    h = _rms(x, gpre[...])
    ag = _mm(h, pw1[...])
    v = ag[:, :d] * jax.nn.sigmoid(ag[:, d:])
    full[:, 0:CONV_PAD, :] = prev_ref[...]
    full[:, CONV_PAD:CONV_PAD + ts, :] = v.reshape(bb, ts, d)

    off = CONV_PAD - (width - 1)
    sb = 4
    for s in range(bb // sb):
        acc = jnp.broadcast_to(dwb[...].reshape(1, 1, d), (sb, ts, d))
        for k in range(width):
            acc = acc + (full[s * sb:(s + 1) * sb, k + off:k + off + ts, :]
                         * dw[k:k + 1, :].reshape(1, 1, d))
        ybuf[s * sb:(s + 1) * sb, :, :] = acc

    y = _layer_norm_silu(ybuf[...].reshape(bb * ts, d), lng[...], lnb[...])
    x = x + _rms(_mm(y, pw2[...]), gpost[...])
    xo_ref[...] = _ffn_ple(x, ple_ref[...], gffpre[...], gffpost[...], wup, wdown, wgate, wproj)
    st_ref[...] = full[:, ts:ts + CONV_PAD, :]


def _kv_kernel(x_ref, cos_ref, sin_ref, gin, gkv, wdkv, *rest, kv_lora, rope_dim, n_heads, up):
    if up:
        wuk, wuv, c_ref, kr_ref, kcat_ref, v_ref = rest
    else:
        c_ref, kr_ref = rest
    h = _rms(x_ref[...], gin[...])
    ckr = _mm(h, wdkv[...])
    c = _rms(ckr[:, :kv_lora], gkv[...])
    c_ref[...] = c
    kr = _rope_chunk(ckr[:, kv_lora:kv_lora + LANES], cos_ref[...], sin_ref[...])
    kr_ref[...] = kr[:, :rope_dim]
    if up:
        cb = c.astype(BF16)
        kn = jnp.dot(cb, wuk[...], preferred_element_type=F32)
        v_ref[...] = jnp.dot(cb, wuv[...], preferred_element_type=F32).astype(BF16)
        nope = kn.shape[1] // n_heads
        krt = kr.T.astype(BF16)
        for hh in range(n_heads):
            base = hh * (nope + LANES)
            kcat_ref[base:base + nope, :] = kn[:, hh * nope:(hh + 1) * nope].T.astype(BF16)
            kcat_ref[base + nope:base + nope + LANES, :] = krt


def _q_kernel(x_ref, cos_ref, sin_ref, gpre, gq, wdq, wuq, *rest, n_heads, nope, scale, absorb):
    h = _rms(x_ref[...], gpre[...])
    cq = _rms(_mm(h, wdq[...]), gq[...])
    q = _mm(cq, wuq[...]) * scale
    cos_t = cos_ref[...]
    sin_t = sin_ref[...]
    if absorb:
        wukt, qlat_ref, qrope_ref = rest
        for hh in range(n_heads):
            qlat_ref[hh] = _mm(q[:, hh * nope:(hh + 1) * nope], wukt[hh])
            rb = n_heads * nope + hh * LANES
            qrope_ref[hh] = _rope_chunk(q[:, rb:rb + LANES], cos_t, sin_t)
    else:
        (qcat_ref,) = rest
        for hh in range(n_heads):
            base = hh * (nope + LANES)
            qcat_ref[:, base:base + nope] = q[:, hh * nope:(hh + 1) * nope].astype(BF16)
            rb = n_heads * nope + hh * LANES
            qcat_ref[:, base + nope:base + nope + LANES] = _rope_chunk(
                q[:, rb:rb + LANES], cos_t, sin_t).astype(BF16)


def _prompt_attn_kernel(q_ref, k_ref, v_ref, o_ref, *, tq):
    t = q_ref.shape[1]
    row = lax.broadcasted_iota(jnp.int32, (tq, tq), 0)
    col = lax.broadcasted_iota(jnp.int32, (tq, tq), 1)
    causal = col <= row
    n_q = t // tq

    def scores(qi):
        lo = qi * tq
        q = q_ref[0, lo:lo + tq, :]
        s_d = jnp.where(causal, jnp.dot(q, k_ref[:, lo:lo + tq], preferred_element_type=F32),
                        -jnp.inf)
        s_o = jnp.dot(q, k_ref[:, 0:lo], preferred_element_type=F32) if qi > 0 else None
        return s_d, s_o

    nxt = scores(0)
    for qi in range(n_q):
        lo = qi * tq
        s_d, s_o = nxt
        if qi + 1 < n_q:
            nxt = scores(qi + 1)
        m = jnp.max(s_d, axis=-1, keepdims=True)
        if qi > 0:
            m = jnp.maximum(m, jnp.max(s_o, axis=-1, keepdims=True))
        p_d = jnp.exp(s_d - m)
        l = jnp.sum(p_d, axis=-1, keepdims=True)
        o = jnp.dot(p_d.astype(BF16), v_ref[0, lo:lo + tq, :], preferred_element_type=F32)
        if qi > 0:
            p_o = jnp.exp(s_o - m)
            l = l + jnp.sum(p_o, axis=-1, keepdims=True)
            o = o + jnp.dot(p_o.astype(BF16), v_ref[0, 0:lo, :], preferred_element_type=F32)
        o_ref[0, lo:lo + tq, :] = (o / l).astype(o_ref.dtype)


def _sample_attn_kernel(pt_ref, qlat_ref, qrope_ref, cnew_ref, krnew_ref, cache_c, cache_krt,
                        o_ref, m_s, l_s, acc_s, qt_s, qrb_s, cbuf, pad_s, kpad_s, cf, kf, sem,
                        *, kp, sub, ts, rope_dim):
    j = pl.program_id(1)
    nj = pl.num_programs(1)
    step = pl.program_id(0) * nj + j
    n_steps = pl.num_programs(0) * nj
    slot = step % 2
    n_heads = qlat_ref.shape[0]
    rows = n_heads * ts
    page = cf.shape[2]

    def page_copies(step_idx, slot_idx):
        copies = []
        for i in range(kp):
            pid = pt_ref[step_idx * kp + i]
            copies.append(pltpu.make_async_copy(cache_c.at[pid], cf.at[slot_idx, i],
                                                sem.at[slot_idx]))
            copies.append(pltpu.make_async_copy(cache_krt.at[pid], kf.at[slot_idx, i],
                                                sem.at[slot_idx]))
        return copies

    @pl.when(step == 0)
    def _():
        for n, cp in enumerate(page_copies(step, slot)):
            cp.start(priority=(n // 2) % 2)

    @pl.when(step + 1 < n_steps)
    def _():
        for n, cp in enumerate(page_copies(step + 1, 1 - slot)):
            cp.start(priority=(n // 2) % 2)

    @pl.when(j == 0)
    def _():
        m_s[...] = jnp.full(m_s.shape, -jnp.inf, F32)
        l_s[...] = jnp.zeros(l_s.shape, F32)
        acc_s[...] = jnp.zeros(acc_s.shape, F32)
        pad_s[...] = jnp.zeros(pad_s.shape, F32)
        pad_s[0:rows, :] = qlat_ref[...].reshape(rows, qlat_ref.shape[-1])
        qt_s[...] = pad_s[...].T.astype(BF16)
        qrb_s[...] = qrope_ref[...].reshape(rows, LANES)[:, :rope_dim].astype(BF16)

    for cp in page_copies(step, slot):
        cp.wait()

    qrb = qrb_s[...]

    def fold(state, s, cb):
        m_old, l, acc = state
        m_new = jnp.maximum(m_old, jnp.max(s, axis=-1, keepdims=True))
        alpha = jnp.exp(m_old - m_new)
        p = jnp.exp(s - m_new)
        l = alpha * l + jnp.sum(p, axis=-1, keepdims=True)
        acc = alpha * acc + jnp.dot(p.astype(BF16), cb, preferred_element_type=F32)
        return m_new, l, acc

    def scores(cb, rope_scores):
        s_t = jnp.dot(cb, qt_s[...], preferred_element_type=F32)
        return s_t.T[:rows, :] + rope_scores

    state = (m_s[...], l_s[...], acc_s[...])
    for i in range(kp):
        cbuf[i * page:(i + 1) * page, :] = cf[slot, i].astype(BF16)
    gk = sub * page
    s_parts = []
    for g in range(kp // sub):
        cb = cbuf[g * gk:(g + 1) * gk, :]
        s_rope = jnp.concatenate(
            [jnp.dot(qrb, kf[slot, i].astype(BF16), preferred_element_type=F32)
             for i in range(g * sub, (g + 1) * sub)], axis=1)
        s_parts.append(scores(cb, s_rope))
    m_old, l, acc = state
    m_new = m_old
    for s in s_parts:
        m_new = jnp.maximum(m_new, jnp.max(s, axis=-1, keepdims=True))
    alpha = jnp.exp(m_old - m_new)
    l = alpha * l
    acc = alpha * acc
    for g, s in enumerate(s_parts):
        p = jnp.exp(s - m_new)
        l = l + jnp.sum(p, axis=-1, keepdims=True)
        acc = acc + jnp.dot(p.astype(BF16), cbuf[g * gk:(g + 1) * gk, :],
                            preferred_element_type=F32)
    state = (m_new, l, acc)

    @pl.when(j < nj - 1)
    def _():
        m_s[...], l_s[...], acc_s[...] = state

    @pl.when(j == nj - 1)
    def _():
        pad_s[...] = jnp.zeros(pad_s.shape, F32)
        kpad_s[...] = jnp.zeros(kpad_s.shape, F32)
        pad_s[0:ts, :] = cnew_ref[...]
        kpad_s[0:ts, :] = krnew_ref[...]
        cb = pad_s[...].astype(BF16)
        s = scores(cb, _dot_nt(qrb, kpad_s[...].astype(BF16)))
        t_q = lax.broadcasted_iota(jnp.int32, s.shape, 0) % ts
        t_k = lax.broadcasted_iota(jnp.int32, s.shape, 1)
        _, l, acc = fold(state, jnp.where(t_k <= t_q, s, -jnp.inf), cb)
        o_ref[...] = (acc / l).reshape(o_ref.shape)


def _post_kernel(x_ref, o_ref, ple_ref, gpost, gffpre, gffpost, wo, *rest, latent):
    if latent:
        wuv, wup, wdown, wgate, wproj, xo_ref = rest
        n_heads, _, vh = wuv.shape
        m = jnp.zeros(x_ref.shape, F32)
        for hh in range(n_heads):
            oh = _mm(o_ref[hh], wuv[hh])
            m = m + _mm(oh, wo[hh * vh:(hh + 1) * vh, :])
    else:
        wup, wdown, wgate, wproj, xo_ref = rest
        m = jnp.dot(o_ref[...], wo[...], preferred_element_type=F32)
    x = x_ref[...] + _rms(m, gpost[...])
    xo_ref[...] = _ffn_ple(x, ple_ref[...], gffpre[...], gffpost[...], wup, wdown, wgate, wproj)


def _wspec(arr):
    nd = arr.ndim
    return pl.BlockSpec(arr.shape, lambda *_: (0,) * nd, pipeline_mode=pl.Buffered(1))


def _params(sem):
    return pltpu.CompilerParams(dimension_semantics=sem, vmem_limit_bytes=VMEM_LIMIT)


def _conv_weights(W, i):
    return [W['g_pre'][i], W['g_post'][i], W['g_ffpre'][i], W['g_ffpost'][i],
            W['pw1'][i], W['dw'][i], W['dwb'][i], W['lng'][i], W['lnb'][i], W['pw2'][i],
            W['w_up'][i], W['w_down'][i], W['ple_gate'][i], W['ple_proj'][i]]


def _conv_layer_prompt(x, ple, prev, W, i, tt=256):
    b, t, d = x.shape
    nt = t // tt
    n_tiles = b * nt
    ws = _conv_weights(W, i)
    width = W['dw'][i].shape[0]

    def mixer_tile(g):
        return jnp.minimum(g, n_tiles - 1)

    def tail_tile(g):
        return jnp.maximum(g - 1, 0)

    return pl.pallas_call(
        functools.partial(_conv_prompt_kernel, tt=tt, nt=nt, width=width),
        grid=(n_tiles + 1,),
        in_specs=[pl.BlockSpec((1, tt, d), lambda g: (mixer_tile(g) // nt, mixer_tile(g) % nt, 0)),
                  pl.BlockSpec((None, tt, ple.shape[-1]), lambda g: (i, tail_tile(g), 0)),
                  pl.BlockSpec((1, CONV_PAD, d), lambda g: (mixer_tile(g) // nt, 0, 0))]
                 + [_wspec(w) for w in ws],
        out_specs=[pl.BlockSpec((1, tt, d), lambda g: (tail_tile(g) // nt, tail_tile(g) % nt, 0)),
                   pl.BlockSpec((1, CONV_PAD, d), lambda g: (mixer_tile(g) // nt, 0, 0))],
        out_shape=[jax.ShapeDtypeStruct((b, t, d), F32),
                   jax.ShapeDtypeStruct((b, CONV_PAD, d), F32)],
        scratch_shapes=[pltpu.VMEM((CONV_PAD + tt, d), F32), pltpu.VMEM((tt, d), F32),
                        pltpu.VMEM((7, tt + CONV_PAD - 8, d), F32),
                        pltpu.VMEM((tt, d), F32)],
        compiler_params=_params(("arbitrary",)),
        name=f"conv_layer_prompt_{i}",
    )(x, ple, prev, *ws)


def _conv_layer_sample(x, ple, prev, W, i, ts, bb=32):
    n, d = x.shape
    tm = bb * ts
    ws = _conv_weights(W, i)
    width = W['dw'][i].shape[0]
    return pl.pallas_call(
        functools.partial(_conv_sample_kernel, ts=ts, width=width),
        grid=(n // tm,),
        in_specs=[pl.BlockSpec((tm, d), lambda g: (g, 0)),
                  pl.BlockSpec((None, tm, ple.shape[-1]), lambda g: (i, g, 0)),
                  pl.BlockSpec((bb, CONV_PAD, d), lambda g: (g, 0, 0))]
                 + [_wspec(w) for w in ws],
        out_specs=[pl.BlockSpec((tm, d), lambda g: (g, 0)),
                   pl.BlockSpec((bb, CONV_PAD, d), lambda g: (g, 0, 0))],
        out_shape=[jax.ShapeDtypeStruct((n, d), F32),
                   jax.ShapeDtypeStruct((n // ts, CONV_PAD, d), F32)],
        scratch_shapes=[pltpu.VMEM((bb, CONV_PAD + ts, d), F32), pltpu.VMEM((bb, ts, d), F32)],
        compiler_params=_params(("arbitrary",)),
        name=f"conv_layer_sample_{i}",
    )(x, ple, prev, *ws)


def _table_spec(tab, tm):
    nblk = tab.shape[0] // tm
    return pl.BlockSpec((tm, LANES), lambda g: (g % nblk, 0))


def _kv_call(x, cos_t, sin_t, W, up, tm=256):
    n, d = x.shape
    kv_lora = W['g_kv'].shape[-1]
    rope_dim = W['rope_dim']
    n_heads = W['n_heads']
    ws = [W['g_kvin'], W['g_kv'], W['w_dkv']] + ([W['w_uk_flat'], W['w_uv_flat']] if up else [])
    out_specs = [pl.BlockSpec((tm, kv_lora), lambda g: (g, 0)),
                 pl.BlockSpec((tm, rope_dim), lambda g: (g, 0))]
    out_shape = [jax.ShapeDtypeStruct((n, kv_lora), F32),
                 jax.ShapeDtypeStruct((n, rope_dim), F32)]
    if up:
        kw = W['w_uk_flat'].shape[1] + n_heads * LANES
        vw = W['w_uv_flat'].shape[1]
        out_specs += [pl.BlockSpec((kw, tm), lambda g: (0, g)),
                      pl.BlockSpec((tm, vw), lambda g: (g, 0))]
        out_shape += [jax.ShapeDtypeStruct((kw, n), BF16), jax.ShapeDtypeStruct((n, vw), BF16)]
    return pl.pallas_call(
        functools.partial(_kv_kernel, kv_lora=kv_lora, rope_dim=rope_dim, n_heads=n_heads, up=up),
        grid=(n // tm,),
        in_specs=[pl.BlockSpec((tm, d), lambda g: (g, 0)),
                  _table_spec(cos_t, tm), _table_spec(sin_t, tm)] + [_wspec(w) for w in ws],
        out_specs=out_specs,
        out_shape=out_shape,
        compiler_params=_params(("parallel",)),
        name="shared_kv_up" if up else "shared_kv",
    )(x, cos_t, sin_t, *ws)


def _q_call(x, cos_t, sin_t, W, i, j, absorb, tm=256):
    n, d = x.shape
    n_heads, nope = W['n_heads'], W['nope']
    ws = [W['g_pre'][i], W['g_q'][j], W['w_dq'][j], W['w_uq'][j]]
    if absorb:
        ws.append(W['w_ukt'])
        kv_lora = W['w_ukt'].shape[-1]
        out_specs = [pl.BlockSpec((n_heads, tm, kv_lora), lambda g: (0, g, 0)),
                     pl.BlockSpec((n_heads, tm, LANES), lambda g: (0, g, 0))]
        out_shape = [jax.ShapeDtypeStruct((n_heads, n, kv_lora), F32),
                     jax.ShapeDtypeStruct((n_heads, n, LANES), F32)]
    else:
        qw = n_heads * (nope + LANES)
        out_specs = [pl.BlockSpec((tm, qw), lambda g: (g, 0))]
        out_shape = [jax.ShapeDtypeStruct((n, qw), BF16)]
    return pl.pallas_call(
        functools.partial(_q_kernel, n_heads=n_heads, nope=nope, scale=W['sm_scale'],
                          absorb=absorb),
        grid=(n // tm,),
        in_specs=[pl.BlockSpec((tm, d), lambda g: (g, 0)),
                  _table_spec(cos_t, tm), _table_spec(sin_t, tm)] + [_wspec(w) for w in ws],
        out_specs=out_specs,
        out_shape=out_shape,
        compiler_params=_params(("parallel",)),
        name=f"mla_query_{'sample' if absorb else 'prompt'}_{j}",
    )(x, cos_t, sin_t, *ws)


def _prompt_attn(qcat, kcat_t, v, n_heads, tq=256):
    b, t, qw = qcat.shape
    hw = qw // n_heads
    vh = v.shape[-1] // n_heads
    return pl.pallas_call(
        functools.partial(_prompt_attn_kernel, tq=tq),
        grid=(b, n_heads),
        in_specs=[pl.BlockSpec((1, t, hw), lambda bi, hi: (bi, 0, hi)),
                  pl.BlockSpec((hw, t), lambda bi, hi: (hi, bi)),
                  pl.BlockSpec((1, t, vh), lambda bi, hi: (bi, 0, hi))],
        out_specs=pl.BlockSpec((1, t, vh), lambda bi, hi: (bi, 0, hi)),
        out_shape=jax.ShapeDtypeStruct((b, t, n_heads * vh), BF16),
        compiler_params=_params(("parallel", "parallel")),
        name="prompt_attention",
    )(qcat, kcat_t, v)


def _sample_attn(qlat, qrope, cnew, krnew, cache_c, cache_krt, page_table, ts, kp=32, sub=16):
    n_heads, n, kv_lora = qlat.shape
    db, n_pages = page_table.shape
    _, rope_dim, page = cache_krt.shape
    nchunk = n_pages // kp
    rows = n_heads * ts
    pt_flat = page_table.reshape(-1)

    in_specs = [pl.BlockSpec((n_heads, ts, kv_lora), lambda b, j, pt: (0, b, 0)),
                pl.BlockSpec((n_heads, ts, LANES), lambda b, j, pt: (0, b, 0)),
                pl.BlockSpec((ts, kv_lora), lambda b, j, pt: (b, 0)),
                pl.BlockSpec((ts, rope_dim), lambda b, j, pt: (b, 0)),
                pl.BlockSpec(memory_space=pl.ANY), pl.BlockSpec(memory_space=pl.ANY)]
    grid_spec = pltpu.PrefetchScalarGridSpec(
        num_scalar_prefetch=1,
        grid=(db, nchunk),
        in_specs=in_specs,
        out_specs=pl.BlockSpec((n_heads, ts, kv_lora), lambda b, j, pt: (0, b, 0)),
        scratch_shapes=[pltpu.VMEM((rows, 1), F32), pltpu.VMEM((rows, 1), F32),
                        pltpu.VMEM((rows, kv_lora), F32),
                        pltpu.VMEM((kv_lora, LANES), BF16), pltpu.VMEM((rows, rope_dim), BF16),
                        pltpu.VMEM((kp * page, kv_lora), BF16),
                        pltpu.VMEM((LANES, kv_lora), F32), pltpu.VMEM((LANES, rope_dim), F32),
                        pltpu.VMEM((2, kp, page, kv_lora), F32),
                        pltpu.VMEM((2, kp, rope_dim, page), F32),
                        pltpu.SemaphoreType.DMA((2,))],
    )
    return pl.pallas_call(
        functools.partial(_sample_attn_kernel, kp=kp, sub=sub, ts=ts, rope_dim=rope_dim),
        grid_spec=grid_spec,
        out_shape=jax.ShapeDtypeStruct((n_heads, n, kv_lora), F32),
        compiler_params=_params(("arbitrary", "arbitrary")),
        name="sample_attention",
    )(pt_flat, qlat, qrope, cnew, krnew, cache_c, cache_krt)


def _post_call(x, o, ple, W, i, j, latent, tm=256):
    n, d = x.shape
    ws = [W['g_post'][i], W['g_ffpre'][i], W['g_ffpost'][i], W['w_o'][j]]
    if latent:
        ws.append(W['w_uv_heads'])
        o_spec = pl.BlockSpec((o.shape[0], tm, o.shape[2]), lambda g: (0, g, 0))
    else:
        o_spec = pl.BlockSpec((tm, o.shape[1]), lambda g: (g, 0))
    ws += [W['w_up'][i], W['w_down'][i], W['ple_gate'][i], W['ple_proj'][i]]
    return pl.pallas_call(
        functools.partial(_post_kernel, latent=latent),
        grid=(n // tm,),
        in_specs=[pl.BlockSpec((tm, d), lambda g: (g, 0)), o_spec,
                  pl.BlockSpec((None, tm, ple.shape[-1]), lambda g: (i, g, 0))]
                 + [_wspec(w) for w in ws],
        out_specs=pl.BlockSpec((tm, d), lambda g: (g, 0)),
        out_shape=jax.ShapeDtypeStruct((n, d), F32),
        compiler_params=_params(("parallel",)),
        name=f"attn_out_mlp_{'sample' if latent else 'prompt'}_{j}",
    )(x, o, ple, *ws)


def _rope_tables(pos, rope_dim):
    half = rope_dim // 2
    freqs = 1.0 / (ROPE_THETA ** (jnp.arange(half, dtype=F32) / half))
    ang = pos.astype(F32)[:, None] * freqs[None, :]
    cos, sin = jnp.cos(ang), jnp.sin(ang)
    z = jnp.zeros((pos.shape[0], LANES - rope_dim), F32)
    return (jnp.concatenate([cos, cos, z], axis=-1),
            jnp.concatenate([-sin, sin, z], axis=-1))


def _prep_weights(norm_mix_pre, norm_mix_post, norm_ff_pre, norm_ff_post,
                  conv_pw1, conv_dw, conv_dw_bias, conv_ln_g, conv_ln_b, conv_pw2,
                  kv_in_norm, w_dkv, kv_norm, w_uk, w_uv, w_dq, q_norm, w_uq, w_o,
                  w_up, w_down, ple_proj, ple_gate):
    depth, d = norm_mix_pre.shape
    kv_lora, n_heads, nope = w_uk.shape
    vh = w_uv.shape[-1]
    rope_dim = w_dkv.shape[1] - kv_lora
    n_b = w_dq.shape[0]
    q_lora = w_dq.shape[-1]
    row = lambda g: [g[i][None, :] for i in range(g.shape[0])]
    bf = lambda w: [w[i].astype(BF16) for i in range(w.shape[0])]

    uq = w_uq.reshape(n_b, q_lora, n_heads, nope + rope_dim)
    uq_nope = uq[..., :nope].reshape(n_b, q_lora, n_heads * nope)
    uq_rope = jnp.pad(uq[..., nope:], ((0, 0), (0, 0), (0, 0), (0, LANES - rope_dim)))
    uq_cat = jnp.concatenate([uq_nope, uq_rope.reshape(n_b, q_lora, n_heads * LANES)], axis=-1)

    return {
        'n_heads': n_heads, 'nope': nope, 'rope_dim': rope_dim,
        'sm_scale': float((nope + rope_dim) ** -0.5),
        'g_pre': row(norm_mix_pre), 'g_post': row(norm_mix_post),
        'g_ffpre': row(norm_ff_pre), 'g_ffpost': row(norm_ff_post),
        'pw1': bf(conv_pw1), 'dw': [conv_dw[i] for i in range(conv_dw.shape[0])],
        'dwb': row(conv_dw_bias), 'lng': row(conv_ln_g), 'lnb': row(conv_ln_b),
        'pw2': bf(conv_pw2),
        'g_kvin': kv_in_norm[None, :], 'g_kv': kv_norm[None, :],
        'w_dkv': jnp.pad(w_dkv, ((0, 0), (0, LANES - rope_dim))).astype(BF16),
        'w_uk_flat': w_uk.reshape(kv_lora, n_heads * nope).astype(BF16),
        'w_uv_flat': w_uv.reshape(kv_lora, n_heads * vh).astype(BF16),
        'w_ukt': jnp.transpose(w_uk, (1, 2, 0)).astype(BF16),
        'w_uv_heads': jnp.transpose(w_uv, (1, 0, 2)).astype(BF16),
        'w_dq': bf(w_dq), 'g_q': row(q_norm), 'w_uq': bf(uq_cat), 'w_o': bf(w_o),
        'w_up': bf(w_up), 'w_down': bf(w_down),
        'ple_gate': bf(ple_gate), 'ple_proj': bf(ple_proj),
    }


def kernel(x_prompt, x_sample, state_conv, cache_kv_latent, cache_k_rope, page_table, p_prompt, p_sample, norm_mix_pre, norm_mix_post, norm_ff_pre, norm_ff_post, conv_pw1, conv_dw, conv_dw_bias, conv_ln_g, conv_ln_b, conv_pw2, kv_in_norm, w_dkv, kv_norm, w_uk, w_uv, w_dq, q_norm, w_uq, w_o, w_up, w_down, ple_proj, ple_gate):
    W = _prep_weights(norm_mix_pre, norm_mix_post, norm_ff_pre, norm_ff_post,
                      conv_pw1, conv_dw, conv_dw_bias, conv_ln_g, conv_ln_b, conv_pw2,
                      kv_in_norm, w_dkv, kv_norm, w_uk, w_uv, w_dq, q_norm, w_uq, w_o,
                      w_up, w_down, ple_proj, ple_gate)
    b, t, d = x_prompt.shape
    db, ts, _ = x_sample.shape
    depth = norm_mix_pre.shape[0]
    n_a = conv_pw1.shape[0]
    n_b = depth - n_a
    width = conv_dw.shape[1]
    n_heads, rope_dim = W['n_heads'], W['rope_dim']
    page = cache_kv_latent.shape[1]
    past_len = page_table.shape[1] * page
    tm, tm_p = 256, 512
    hist_pad = ((0, 0), (CONV_PAD - (width - 1), 0), (0, 0))

    cos_p, sin_p = _rope_tables(jnp.arange(t), rope_dim)
    ple_p = p_prompt.reshape(depth, b * t, -1)
    x = x_prompt
    conv_p = []
    zero_prev = jnp.zeros((b, CONV_PAD, d), F32)
    for i in range(n_a):
        x, st = _conv_layer_prompt(x, ple_p, zero_prev, W, i)
        conv_p.append(st[:, CONV_PAD - (width - 1):, :])
    x = x.reshape(b * t, d)
    c_p, kr_p, kcat, v = _kv_call(x, cos_p, sin_p, W, up=True, tm=tm_p)
    v = v.reshape(b, t, -1)
    for j in range(n_b):
        i = n_a + j
        (qcat,) = _q_call(x, cos_p, sin_p, W, i, j, absorb=False, tm=tm_p)
        o = _prompt_attn(qcat.reshape(b, t, -1), kcat, v, n_heads)
        x = _post_call(x, o.reshape(b * t, -1), ple_p, W, i, j, latent=False, tm=tm_p)
    y_prompt = x.reshape(b, t, d)

    pos_s = past_len + jnp.arange(ts)
    cos_s, sin_s = _rope_tables(jnp.tile(pos_s, tm // ts), rope_dim)
    ple_s = p_sample.reshape(depth, db * ts, -1)
    x = x_sample.reshape(db * ts, d)
    conv_s = []
    for i in range(n_a):
        x, st = _conv_layer_sample(x, ple_s, jnp.pad(state_conv[i], hist_pad), W, i, ts)
        conv_s.append(st[:, CONV_PAD - (width - 1):, :])
    c_s, kr_s = _kv_call(x, cos_s, sin_s, W, up=False, tm=tm)
    cache_krt = jnp.swapaxes(cache_k_rope, 1, 2)
    for j in range(n_b):
        i = n_a + j
        qlat, qrope = _q_call(x, cos_s, sin_s, W, i, j, absorb=True, tm=tm)
        o_lat = _sample_attn(qlat, qrope, c_s, kr_s, cache_kv_latent, cache_krt, page_table, ts)
        x = _post_call(x, o_lat, ple_s, W, i, j, latent=True, tm=tm)
    y_sample = x.reshape(db, ts, d)

    return (y_prompt, y_sample, jnp.stack(conv_p), jnp.stack(conv_s),
            c_p.reshape(b, t, -1), kr_p.reshape(b, t, -1),
            c_s.reshape(db, ts, -1), kr_s.reshape(db, ts, -1))
```

```python
import functools

import jax
import jax.numpy as jnp
from jax import lax
from jax.experimental import pallas as pl
from jax.experimental.pallas import tpu as pltpu

F32 = jnp.float32
BF16 = jnp.bfloat16

EPS = 1e-6
ROPE_THETA = 10000.0
LANES = 128
VMEM_LIMIT = 56 * 1024 * 1024

CONV_PAD = 32
FF_CHUNK = 1024
PAGE_SLOTS = 3


def _rms(x, g):
    return x * lax.rsqrt(jnp.mean(x * x, axis=-1, keepdims=True) + EPS) * g


def _mm(a, w):
    return jnp.dot(a.astype(BF16), w, preferred_element_type=F32)


def _dot_nt(a, b):
    return lax.dot_general(a, b, (((1,), (1,)), ((), ())), preferred_element_type=F32)


def _rope_chunk(chunk, cos_t, sin_t):
    half = chunk.shape[-1] // 4
    lane = lax.broadcasted_iota(jnp.int32, chunk.shape, 1)
    swapped = jnp.where(lane < half,
                        pltpu.roll(chunk, LANES - half, 1),
                        pltpu.roll(chunk, half, 1))
    return chunk * cos_t + swapped * sin_t


def _ffn_ple(x, ple, g_ffpre, g_ffpost, wup, wdown, wgate, wproj):
    d_ff = wup.shape[1]
    h = _rms(x, g_ffpre).astype(BF16)
    acc = jnp.zeros_like(x)
    for c in range(d_ff // FF_CHUNK):
        u = jnp.dot(h, wup[:, c * FF_CHUNK:(c + 1) * FF_CHUNK], preferred_element_type=F32)
        u = jnp.square(jnp.maximum(u, 0.0)).astype(BF16)
        acc = acc + jnp.dot(u, wdown[c * FF_CHUNK:(c + 1) * FF_CHUNK, :],
                            preferred_element_type=F32)
    x = x + _rms(acc, g_ffpost)
    gate = jax.nn.sigmoid(_mm(x, wgate[...]))
    return x + gate * _mm(ple, wproj[...])


def _layer_norm_silu(y, g, b):
    mu = jnp.mean(y, axis=-1, keepdims=True)
    yc = y - mu
    var = jnp.mean(yc * yc, axis=-1, keepdims=True)
    z = yc * lax.rsqrt(var + EPS) * g + b
    return z * jax.nn.sigmoid(z)


def _conv_prompt_kernel(x_ref, ple_ref, prev_ref, gpre, gpost, gffpre, gffpost,
                        pw1, dw, dwb, lng, lnb, pw2, wup, wdown, wgate, wproj,
                        xo_ref, st_ref, buf, ybuf, sh, xm_prev, *, tt, nt, width):
    g = pl.program_id(0)
    n_tiles = pl.num_programs(0) - 1
    t = jnp.minimum(g, n_tiles - 1) % nt
    d = x_ref.shape[-1]

    @pl.when(g == 0)
    def _():
        xm_prev[...] = jnp.zeros(xm_prev.shape, F32)

    @pl.when(t == 0)
    def _():
        buf[0:CONV_PAD, :] = prev_ref[0]

    x = x_ref[0]
    h = _rms(x, gpre[...])
    ag = _mm(h, pw1[...])
    buf[CONV_PAD:CONV_PAD + tt, :] = ag[:, :d] * jax.nn.sigmoid(ag[:, d:])
    off = CONV_PAD - (width - 1)
    n_sh = sh.shape[1]
    for s in range(1, 8):
        sh[s - 1] = buf[s:s + n_sh, :]
    rc, lc = 64, 256

    def conv_rows(r):
        for c in range(d // lc):
            cs = slice(c * lc, (c + 1) * lc)
            acc = jnp.broadcast_to(dwb[:, cs], (rc, lc))
            for k in range(width):
                s = (k + off) % 8
                lo = r * rc + (k + off) - s
                tap = buf[lo:lo + rc, cs] if s == 0 else sh[s - 1, lo:lo + rc, cs]
                acc = acc + tap * dw[k:k + 1, cs]
            ybuf[r * rc:(r + 1) * rc, cs] = acc

    xp = xm_prev[...]
    hp = _rms(xp, gffpre[...]).astype(BF16)
    n_ff = wup.shape[1] // FF_CHUNK
    n_rb = tt // rc
    acc_ff = jnp.zeros_like(xp)
    for c in range(n_ff):
        u = jnp.dot(hp, wup[:, c * FF_CHUNK:(c + 1) * FF_CHUNK], preferred_element_type=F32)
        u = jnp.square(jnp.maximum(u, 0.0)).astype(BF16)
        acc_ff = acc_ff + jnp.dot(u, wdown[c * FF_CHUNK:(c + 1) * FF_CHUNK, :],
                                  preferred_element_type=F32)
        for r in range(c * n_rb // n_ff, (c + 1) * n_rb // n_ff):
            conv_rows(r)
    xp = xp + _rms(acc_ff, gffpost[...])
    gate = jax.nn.sigmoid(_mm(xp, wgate[...]))
    xo_ref[0] = xp + gate * _mm(ple_ref[...], wproj[...])

    y = _layer_norm_silu(ybuf[...], lng[...], lnb[...])
    xm_prev[...] = x + _rms(_mm(y, pw2[...]), gpost[...])

    tail = buf[tt:tt + CONV_PAD, :]
    buf[0:CONV_PAD, :] = tail

    @pl.when(t == nt - 1)
    def _():
        st_ref[0] = tail


def _conv_sample_kernel(x_ref, ple_ref, prev_ref, gpre, gpost, gffpre, gffpost,
                        pw1, dw, dwb, lng, lnb, pw2, wup, wdown, wgate, wproj,
                        xo_ref, st_ref, full, ybuf, *, ts, width):
    d = x_ref.shape[-1]
    bb = prev_ref.shape[0]
    x = x_ref[...]
    h = _rms(x, gpre[...])
    ag = _mm(h, pw1[...])
    v = ag[:, :d] * jax.nn.sigmoid(ag[:, d:])
    full[:, 0:CONV_PAD, :] = prev_ref[...]
    full[:, CONV_PAD:CONV_PAD + ts, :] = v.reshape(bb, ts, d)

    off = CONV_PAD - (width - 1)
    sb = 4
    for s in range(bb // sb):
        acc = jnp.broadcast_to(dwb[...].reshape(1, 1, d), (sb, ts, d))
        for k in range(width):
            acc = acc + (full[s * sb:(s + 1) * sb, k + off:k + off + ts, :]
                         * dw[k:k + 1, :].reshape(1, 1, d))
        ybuf[s * sb:(s + 1) * sb, :, :] = acc

    y = _layer_norm_silu(ybuf[...].reshape(bb * ts, d), lng[...], lnb[...])
    x = x + _rms(_mm(y, pw2[...]), gpost[...])
    xo_ref[...] = _ffn_ple(x, ple_ref[...], gffpre[...], gffpost[...], wup, wdown, wgate, wproj)
    st_ref[...] = full[:, ts:ts + CONV_PAD, :]


def _kv_kernel(x_ref, cos_ref, sin_ref, gin, gkv, wdkv, *rest, kv_lora, rope_dim, n_heads, up):
    if up:
        wuk, wuv, c_ref, kr_ref, kcat_ref, v_ref = rest
    else:
        c_ref, kr_ref = rest
    h = _rms(x_ref[...], gin[...])
    ckr = _mm(h, wdkv[...])
    c = _rms(ckr[:, :kv_lora], gkv[...])
    c_ref[...] = c
    kr = _rope_chunk(ckr[:, kv_lora:kv_lora + LANES], cos_ref[...], sin_ref[...])
    kr_ref[...] = kr[:, :rope_dim]
    if up:
        cb = c.astype(BF16)
        kn = jnp.dot(cb, wuk[...], preferred_element_type=F32)
        v_ref[...] = jnp.dot(cb, wuv[...], preferred_element_type=F32).astype(BF16)
        nope = kn.shape[1] // n_heads
        krt = kr.T.astype(BF16)
        for hh in range(n_heads):
            base = hh * (nope + LANES)
            kcat_ref[base:base + nope, :] = kn[:, hh * nope:(hh + 1) * nope].T.astype(BF16)
            kcat_ref[base + nope:base + nope + LANES, :] = krt


def _q_kernel(x_ref, cos_ref, sin_ref, gpre, gq, wdq, wuq, *rest, n_heads, nope, scale, absorb):
    h = _rms(x_ref[...], gpre[...])
    cq = _rms(_mm(h, wdq[...]), gq[...])
    q = _mm(cq, wuq[...]) * scale
    cos_t = cos_ref[...]
    sin_t = sin_ref[...]
    if absorb:
        wukt, qlat_ref, qrope_ref = rest
        for hh in range(n_heads):
            qlat_ref[hh] = _mm(q[:, hh * nope:(hh + 1) * nope], wukt[hh])
            rb = n_heads * nope + hh * LANES
            qrope_ref[hh] = _rope_chunk(q[:, rb:rb + LANES], cos_t, sin_t)
    else:
        (qcat_ref,) = rest
        for hh in range(n_heads):
            base = hh * (nope + LANES)
            qcat_ref[:, base:base + nope] = q[:, hh * nope:(hh + 1) * nope].astype(BF16)
            rb = n_heads * nope + hh * LANES
            qcat_ref[:, base + nope:base + nope + LANES] = _rope_chunk(
                q[:, rb:rb + LANES], cos_t, sin_t).astype(BF16)


def _prompt_attn_kernel(q_ref, k_ref, v_ref, o_ref, *, tq):
    t = q_ref.shape[1]
    row = lax.broadcasted_iota(jnp.int32, (tq, tq), 0)
    col = lax.broadcasted_iota(jnp.int32, (tq, tq), 1)
    causal = col <= row
    n_q = t // tq

    def scores(qi):
        lo = qi * tq
        q = q_ref[0, lo:lo + tq, :]
        s_d = jnp.where(causal, jnp.dot(q, k_ref[:, lo:lo + tq], preferred_element_type=F32),
                        -jnp.inf)
        s_o = jnp.dot(q, k_ref[:, 0:lo], preferred_element_type=F32) if qi > 0 else None
        return s_d, s_o

    nxt = scores(0)
    for qi in range(n_q):
        lo = qi * tq
        s_d, s_o = nxt
        if qi + 1 < n_q:
            nxt = scores(qi + 1)
        m = jnp.max(s_d, axis=-1, keepdims=True)
        if qi > 0:
            m = jnp.maximum(m, jnp.max(s_o, axis=-1, keepdims=True))
        p_d = jnp.exp(s_d - m)
        l = jnp.sum(p_d, axis=-1, keepdims=True)
        o = jnp.dot(p_d.astype(BF16), v_ref[0, lo:lo + tq, :], preferred_element_type=F32)
        if qi > 0:
            p_o = jnp.exp(s_o - m)
            l = l + jnp.sum(p_o, axis=-1, keepdims=True)
            o = o + jnp.dot(p_o.astype(BF16), v_ref[0, 0:lo, :], preferred_element_type=F32)
        o_ref[0, lo:lo + tq, :] = (o / l).astype(o_ref.dtype)


def _sample_attn_kernel(pt_ref, qlat_ref, qrope_ref, cnew_ref, krnew_ref, cache_c, cache_krt,
                        o_ref, m_s, l_s, acc_s, qt_s, qrb_s, cbuf, pad_s, kpad_s, cf, kf, sem,
                        *, kp, sub, ts, rope_dim):
    j = pl.program_id(1)
    nj = pl.num_programs(1)
    step = pl.program_id(0) * nj + j
    n_steps = pl.num_programs(0) * nj
    n_slots = cf.shape[0]
    slot = step % n_slots
    n_heads = qlat_ref.shape[0]
    rows = n_heads * ts
    page = cf.shape[2]

    def page_copies(step_idx, slot_idx):
        copies = []
        for i in range(kp):
            pid = pt_ref[step_idx * kp + i]
            copies.append(pltpu.make_async_copy(cache_c.at[pid], cf.at[slot_idx, i],
                                                sem.at[slot_idx]))
            copies.append(pltpu.make_async_copy(cache_krt.at[pid], kf.at[slot_idx, i],
                                                sem.at[slot_idx]))
        return copies

    ahead = n_slots - 1

    def start_fetch(step_idx):
        for n, cp in enumerate(page_copies(step_idx, step_idx % n_slots)):
            cp.start(priority=(n // 2) % 2)

    @pl.when(step == 0)
    def _():
        for s0 in range(ahead):
            @pl.when(s0 < n_steps)
            def _():
                start_fetch(s0)

    @pl.when(step + ahead < n_steps)
    def _():
        start_fetch(step + ahead)

    @pl.when(j == 0)
    def _():
        m_s[...] = jnp.full(m_s.shape, -jnp.inf, F32)
        l_s[...] = jnp.zeros(l_s.shape, F32)
        acc_s[...] = jnp.zeros(acc_s.shape, F32)
        pad_s[...] = jnp.zeros(pad_s.shape, F32)
        pad_s[0:rows, :] = qlat_ref[...].reshape(rows, qlat_ref.shape[-1])
        qt_s[...] = pad_s[...].T.astype(BF16)
        qrb_s[...] = qrope_ref[...].reshape(rows, LANES)[:, :rope_dim].astype(BF16)

    for cp in page_copies(step, slot):
        cp.wait()

    qrb = qrb_s[...]

    def fold(state, s, cb):
        m_old, l, acc = state
        m_new = jnp.maximum(m_old, jnp.max(s, axis=-1, keepdims=True))
        alpha = jnp.exp(m_old - m_new)
        p = jnp.exp(s - m_new)
        l = alpha * l + jnp.sum(p, axis=-1, keepdims=True)
        acc = alpha * acc + jnp.dot(p.astype(BF16), cb, preferred_element_type=F32)
        return m_new, l, acc

    def scores(cb, rope_scores):
        s_t = jnp.dot(cb, qt_s[...], preferred_element_type=F32)
        return s_t.T[:rows, :] + rope_scores

    state = (m_s[...], l_s[...], acc_s[...])
    for i in range(kp):
        cbuf[i * page:(i + 1) * page, :] = cf[slot, i].astype(BF16)
    gk = sub * page
    s_parts = []
    for g in range(kp // sub):
        cb = cbuf[g * gk:(g + 1) * gk, :]
        s_rope = jnp.concatenate(
            [jnp.dot(qrb, kf[slot, i].astype(BF16), preferred_element_type=F32)
             for i in range(g * sub, (g + 1) * sub)], axis=1)
        s_parts.append(scores(cb, s_rope))
    m_old, l, acc = state
    m_new = m_old
    for s in s_parts:
        m_new = jnp.maximum(m_new, jnp.max(s, axis=-1, keepdims=True))
    alpha = jnp.exp(m_old - m_new)
    l = alpha * l
    acc = alpha * acc
    for g, s in enumerate(s_parts):
        p = jnp.exp(s - m_new)
        l = l + jnp.sum(p, axis=-1, keepdims=True)
        acc = acc + jnp.dot(p.astype(BF16), cbuf[g * gk:(g + 1) * gk, :],
                            preferred_element_type=F32)
    state = (m_new, l, acc)

    @pl.when(j < nj - 1)
    def _():
        m_s[...], l_s[...], acc_s[...] = state

    @pl.when(j == nj - 1)
    def _():
        pad_s[...] = jnp.zeros(pad_s.shape, F32)
        kpad_s[...] = jnp.zeros(kpad_s.shape, F32)
        pad_s[0:ts, :] = cnew_ref[...]
        kpad_s[0:ts, :] = krnew_ref[...]
        cb = pad_s[...].astype(BF16)
        s = scores(cb, _dot_nt(qrb, kpad_s[...].astype(BF16)))
        t_q = lax.broadcasted_iota(jnp.int32, s.shape, 0) % ts
        t_k = lax.broadcasted_iota(jnp.int32, s.shape, 1)
        _, l, acc = fold(state, jnp.where(t_k <= t_q, s, -jnp.inf), cb)
        o_ref[...] = (acc / l).reshape(o_ref.shape)


def _post_kernel(x_ref, o_ref, ple_ref, gpost, gffpre, gffpost, wo, *rest, latent):
    if latent:
        wuv, wup, wdown, wgate, wproj, xo_ref = rest
        n_heads, _, vh = wuv.shape
        m = jnp.zeros(x_ref.shape, F32)
        for hh in range(n_heads):
            oh = _mm(o_ref[hh], wuv[hh])
            m = m + _mm(oh, wo[hh * vh:(hh + 1) * vh, :])
    else:
        wup, wdown, wgate, wproj, xo_ref = rest
        m = jnp.dot(o_ref[...], wo[...], preferred_element_type=F32)
    x = x_ref[...] + _rms(m, gpost[...])
    xo_ref[...] = _ffn_ple(x, ple_ref[...], gffpre[...], gffpost[...], wup, wdown, wgate, wproj)


def _wspec(w):
    if isinstance(w, tuple):
        arr, layer = w
        rest = arr.ndim - 1
        return pl.BlockSpec((None,) + arr.shape[1:], lambda *_: (layer,) + (0,) * rest,
                            pipeline_mode=pl.Buffered(1))
    nd = w.ndim
    return pl.BlockSpec(w.shape, lambda *_: (0,) * nd, pipeline_mode=pl.Buffered(1))


def _warg(w):
    return w[0] if isinstance(w, tuple) else w


def _params(sem):
    return pltpu.CompilerParams(dimension_semantics=sem, vmem_limit_bytes=VMEM_LIMIT)


def _conv_weights(W, i):
    names = ['g_pre', 'g_post', 'g_ffpre', 'g_ffpost', 'pw1', 'dw', 'dwb', 'lng', 'lnb', 'pw2',
             'w_up', 'w_down', 'ple_gate', 'ple_proj']
    return [(W[n], i) for n in names]


def _conv_layer_prompt(x, ple, prev, W, i, tt=256):
    b, t, d = x.shape
    nt = t // tt
    n_tiles = b * nt
    ws = _conv_weights(W, i)
    width = W['dw'].shape[1]

    def mixer_tile(g):
        return jnp.minimum(g, n_tiles - 1)

    def tail_tile(g):
        return jnp.maximum(g - 1, 0)

    return pl.pallas_call(
        functools.partial(_conv_prompt_kernel, tt=tt, nt=nt, width=width),
        grid=(n_tiles + 1,),
        in_specs=[pl.BlockSpec((1, tt, d), lambda g: (mixer_tile(g) // nt, mixer_tile(g) % nt, 0)),
                  pl.BlockSpec((None, tt, ple.shape[-1]), lambda g: (i, tail_tile(g), 0)),
                  pl.BlockSpec((1, CONV_PAD, d), lambda g: (mixer_tile(g) // nt, 0, 0))]
                 + [_wspec(w) for w in ws],
        out_specs=[pl.BlockSpec((1, tt, d), lambda g: (tail_tile(g) // nt, tail_tile(g) % nt, 0)),
                   pl.BlockSpec((1, CONV_PAD, d), lambda g: (mixer_tile(g) // nt, 0, 0))],
        out_shape=[jax.ShapeDtypeStruct((b, t, d), F32),
                   jax.ShapeDtypeStruct((b, CONV_PAD, d), F32)],
        scratch_shapes=[pltpu.VMEM((CONV_PAD + tt, d), F32), pltpu.VMEM((tt, d), F32),
                        pltpu.VMEM((7, tt + CONV_PAD - 8, d), F32),
                        pltpu.VMEM((tt, d), F32)],
        compiler_params=_params(("arbitrary",)),
        name=f"conv_layer_prompt_{i}",
    )(x, ple, prev, *[_warg(w) for w in ws])


def _conv_layer_sample(x, ple, prev, W, i, ts, bb=32):
    n, d = x.shape
    tm = bb * ts
    ws = _conv_weights(W, i)
    width = W['dw'].shape[1]
    return pl.pallas_call(
        functools.partial(_conv_sample_kernel, ts=ts, width=width),
        grid=(n // tm,),
        in_specs=[pl.BlockSpec((tm, d), lambda g: (g, 0)),
                  pl.BlockSpec((None, tm, ple.shape[-1]), lambda g: (i, g, 0)),
                  pl.BlockSpec((bb, CONV_PAD, d), lambda g: (g, 0, 0))]
                 + [_wspec(w) for w in ws],
        out_specs=[pl.BlockSpec((tm, d), lambda g: (g, 0)),
                   pl.BlockSpec((bb, CONV_PAD, d), lambda g: (g, 0, 0))],
        out_shape=[jax.ShapeDtypeStruct((n, d), F32),
                   jax.ShapeDtypeStruct((n // ts, CONV_PAD, d), F32)],
        scratch_shapes=[pltpu.VMEM((bb, CONV_PAD + ts, d), F32), pltpu.VMEM((bb, ts, d), F32)],
        compiler_params=_params(("arbitrary",)),
        name=f"conv_layer_sample_{i}",
    )(x, ple, prev, *[_warg(w) for w in ws])


def _table_spec(tab, tm):
    nblk = tab.shape[0] // tm
    return pl.BlockSpec((tm, LANES), lambda g: (g % nblk, 0))


def _kv_call(x, cos_t, sin_t, W, up, tm=256):
    n, d = x.shape
    kv_lora = W['g_kv'].shape[-1]
    rope_dim = W['rope_dim']
    n_heads = W['n_heads']
    ws = [W['g_kvin'], W['g_kv'], W['w_dkv']] + ([W['w_uk_flat'], W['w_uv_flat']] if up else [])
    out_specs = [pl.BlockSpec((tm, kv_lora), lambda g: (g, 0)),
                 pl.BlockSpec((tm, rope_dim), lambda g: (g, 0))]
    out_shape = [jax.ShapeDtypeStruct((n, kv_lora), F32),
                 jax.ShapeDtypeStruct((n, rope_dim), F32)]
    if up:
        kw = W['w_uk_flat'].shape[1] + n_heads * LANES
        vw = W['w_uv_flat'].shape[1]
        out_specs += [pl.BlockSpec((kw, tm), lambda g: (0, g)),
                      pl.BlockSpec((tm, vw), lambda g: (g, 0))]
        out_shape += [jax.ShapeDtypeStruct((kw, n), BF16), jax.ShapeDtypeStruct((n, vw), BF16)]
    return pl.pallas_call(
        functools.partial(_kv_kernel, kv_lora=kv_lora, rope_dim=rope_dim, n_heads=n_heads, up=up),
        grid=(n // tm,),
        in_specs=[pl.BlockSpec((tm, d), lambda g: (g, 0)),
                  _table_spec(cos_t, tm), _table_spec(sin_t, tm)] + [_wspec(w) for w in ws],
        out_specs=out_specs,
        out_shape=out_shape,
        compiler_params=_params(("parallel",)),
        name="shared_kv_up" if up else "shared_kv",
    )(x, cos_t, sin_t, *[_warg(w) for w in ws])


def _q_call(x, cos_t, sin_t, W, i, j, absorb, tm=256):
    n, d = x.shape
    n_heads, nope = W['n_heads'], W['nope']
    ws = [(W['g_pre'], i), (W['g_q'], j), (W['w_dq'], j), (W['w_uq'], j)]
    if absorb:
        ws.append(W['w_ukt'])
        kv_lora = W['w_ukt'].shape[-1]
        out_specs = [pl.BlockSpec((n_heads, tm, kv_lora), lambda g: (0, g, 0)),
                     pl.BlockSpec((n_heads, tm, LANES), lambda g: (0, g, 0))]
        out_shape = [jax.ShapeDtypeStruct((n_heads, n, kv_lora), F32),
                     jax.ShapeDtypeStruct((n_heads, n, LANES), F32)]
    else:
        qw = n_heads * (nope + LANES)
        out_specs = [pl.BlockSpec((tm, qw), lambda g: (g, 0))]
        out_shape = [jax.ShapeDtypeStruct((n, qw), BF16)]
    return pl.pallas_call(
        functools.partial(_q_kernel, n_heads=n_heads, nope=nope, scale=W['sm_scale'],
                          absorb=absorb),
        grid=(n // tm,),
        in_specs=[pl.BlockSpec((tm, d), lambda g: (g, 0)),
                  _table_spec(cos_t, tm), _table_spec(sin_t, tm)] + [_wspec(w) for w in ws],
        out_specs=out_specs,
        out_shape=out_shape,
        compiler_params=_params(("parallel",)),
        name=f"mla_query_{'sample' if absorb else 'prompt'}_{j}",
    )(x, cos_t, sin_t, *[_warg(w) for w in ws])


def _prompt_attn(qcat, kcat_t, v, n_heads, tq=256):
    b, t, qw = qcat.shape
    hw = qw // n_heads
    vh = v.shape[-1] // n_heads
    return pl.pallas_call(
        functools.partial(_prompt_attn_kernel, tq=tq),
        grid=(b, n_heads),
        in_specs=[pl.BlockSpec((1, t, hw), lambda bi, hi: (bi, 0, hi)),
                  pl.BlockSpec((hw, t), lambda bi, hi: (hi, bi)),
                  pl.BlockSpec((1, t, vh), lambda bi, hi: (bi, 0, hi))],
        out_specs=pl.BlockSpec((1, t, vh), lambda bi, hi: (bi, 0, hi)),
        out_shape=jax.ShapeDtypeStruct((b, t, n_heads * vh), BF16),
        compiler_params=_params(("parallel", "parallel")),
        name="prompt_attention",
    )(qcat, kcat_t, v)


def _sample_attn(qlat, qrope, cnew, krnew, cache_c, cache_krt, page_table, ts, kp=32, sub=16):
    n_heads, n, kv_lora = qlat.shape
    db, n_pages = page_table.shape
    _, rope_dim, page = cache_krt.shape
    nchunk = n_pages // kp
    rows = n_heads * ts
    pt_flat = page_table.reshape(-1)

    in_specs = [pl.BlockSpec((n_heads, ts, kv_lora), lambda b, j, pt: (0, b, 0)),
                pl.BlockSpec((n_heads, ts, LANES), lambda b, j, pt: (0, b, 0)),
                pl.BlockSpec((ts, kv_lora), lambda b, j, pt: (b, 0)),
                pl.BlockSpec((ts, rope_dim), lambda b, j, pt: (b, 0)),
                pl.BlockSpec(memory_space=pl.ANY), pl.BlockSpec(memory_space=pl.ANY)]
    grid_spec = pltpu.PrefetchScalarGridSpec(
        num_scalar_prefetch=1,
        grid=(db, nchunk),
        in_specs=in_specs,
        out_specs=pl.BlockSpec((n_heads, ts, kv_lora), lambda b, j, pt: (0, b, 0)),
        scratch_shapes=[pltpu.VMEM((rows, 1), F32), pltpu.VMEM((rows, 1), F32),
                        pltpu.VMEM((rows, kv_lora), F32),
                        pltpu.VMEM((kv_lora, LANES), BF16), pltpu.VMEM((rows, rope_dim), BF16),
                        pltpu.VMEM((kp * page, kv_lora), BF16),
                        pltpu.VMEM((LANES, kv_lora), F32), pltpu.VMEM((LANES, rope_dim), F32),
                        pltpu.VMEM((PAGE_SLOTS, kp, page, kv_lora), F32),
                        pltpu.VMEM((PAGE_SLOTS, kp, rope_dim, page), F32),
                        pltpu.SemaphoreType.DMA((PAGE_SLOTS,))],
    )
    return pl.pallas_call(
        functools.partial(_sample_attn_kernel, kp=kp, sub=sub, ts=ts, rope_dim=rope_dim),
        grid_spec=grid_spec,
        out_shape=jax.ShapeDtypeStruct((n_heads, n, kv_lora), F32),
        compiler_params=_params(("arbitrary", "arbitrary")),
        name="sample_attention",
    )(pt_flat, qlat, qrope, cnew, krnew, cache_c, cache_krt)


def _post_call(x, o, ple, W, i, j, latent, tm=256):
    n, d = x.shape
    ws = [(W['g_post'], i), (W['g_ffpre'], i), (W['g_ffpost'], i), (W['w_o'], j)]
    if latent:
        ws.append(W['w_uv_heads'])
        o_spec = pl.BlockSpec((o.shape[0], tm, o.shape[2]), lambda g: (0, g, 0))
    else:
        o_spec = pl.BlockSpec((tm, o.shape[1]), lambda g: (g, 0))
    ws += [(W['w_up'], i), (W['w_down'], i), (W['ple_gate'], i), (W['ple_proj'], i)]
    return pl.pallas_call(
        functools.partial(_post_kernel, latent=latent),
        grid=(n // tm,),
        in_specs=[pl.BlockSpec((tm, d), lambda g: (g, 0)), o_spec,
                  pl.BlockSpec((None, tm, ple.shape[-1]), lambda g: (i, g, 0))]
                 + [_wspec(w) for w in ws],
        out_specs=pl.BlockSpec((tm, d), lambda g: (g, 0)),
        out_shape=jax.ShapeDtypeStruct((n, d), F32),
        compiler_params=_params(("parallel",)),
        name=f"attn_out_mlp_{'sample' if latent else 'prompt'}_{j}",
    )(x, o, ple, *[_warg(w) for w in ws])


def _rope_tables(pos, rope_dim):
    half = rope_dim // 2
    freqs = 1.0 / (ROPE_THETA ** (jnp.arange(half, dtype=F32) / half))
    ang = pos.astype(F32)[:, None] * freqs[None, :]
    cos, sin = jnp.cos(ang), jnp.sin(ang)
    z = jnp.zeros((pos.shape[0], LANES - rope_dim), F32)
    return (jnp.concatenate([cos, cos, z], axis=-1),
            jnp.concatenate([-sin, sin, z], axis=-1))


def _prep_weights(norm_mix_pre, norm_mix_post, norm_ff_pre, norm_ff_post,
                  conv_pw1, conv_dw, conv_dw_bias, conv_ln_g, conv_ln_b, conv_pw2,
                  kv_in_norm, w_dkv, kv_norm, w_uk, w_uv, w_dq, q_norm, w_uq, w_o,
                  w_up, w_down, ple_proj, ple_gate):
    depth, d = norm_mix_pre.shape
    kv_lora, n_heads, nope = w_uk.shape
    vh = w_uv.shape[-1]
    rope_dim = w_dkv.shape[1] - kv_lora
    n_b = w_dq.shape[0]
    q_lora = w_dq.shape[-1]
    row = lambda g: g[:, None, :]
    bf = lambda w: w.astype(BF16)

    uq = w_uq.reshape(n_b, q_lora, n_heads, nope + rope_dim)
    uq_nope = uq[..., :nope].reshape(n_b, q_lora, n_heads * nope)
    uq_rope = jnp.pad(uq[..., nope:], ((0, 0), (0, 0), (0, 0), (0, LANES - rope_dim)))
    uq_cat = jnp.concatenate([uq_nope, uq_rope.reshape(n_b, q_lora, n_heads * LANES)], axis=-1)

    return {
        'n_heads': n_heads, 'nope': nope, 'rope_dim': rope_dim,
        'sm_scale': float((nope + rope_dim) ** -0.5),
        'g_pre': row(norm_mix_pre), 'g_post': row(norm_mix_post),
        'g_ffpre': row(norm_ff_pre), 'g_ffpost': row(norm_ff_post),
        'pw1': bf(conv_pw1), 'dw': conv_dw,
        'dwb': row(conv_dw_bias), 'lng': row(conv_ln_g), 'lnb': row(conv_ln_b),
        'pw2': bf(conv_pw2),
        'g_kvin': kv_in_norm[None, :], 'g_kv': kv_norm[None, :],
        'w_dkv': jnp.pad(w_dkv, ((0, 0), (0, LANES - rope_dim))).astype(BF16),
        'w_uk_flat': w_uk.reshape(kv_lora, n_heads * nope).astype(BF16),
        'w_uv_flat': w_uv.reshape(kv_lora, n_heads * vh).astype(BF16),
        'w_ukt': jnp.transpose(w_uk, (1, 2, 0)).astype(BF16),
        'w_uv_heads': jnp.transpose(w_uv, (1, 0, 2)).astype(BF16),
        'w_dq': bf(w_dq), 'g_q': row(q_norm), 'w_uq': bf(uq_cat), 'w_o': bf(w_o),
        'w_up': bf(w_up), 'w_down': bf(w_down),
        'ple_gate': bf(ple_gate), 'ple_proj': bf(ple_proj),
    }


def kernel(x_prompt, x_sample, state_conv, cache_kv_latent, cache_k_rope, page_table, p_prompt, p_sample, norm_mix_pre, norm_mix_post, norm_ff_pre, norm_ff_post, conv_pw1, conv_dw, conv_dw_bias, conv_ln_g, conv_ln_b, conv_pw2, kv_in_norm, w_dkv, kv_norm, w_uk, w_uv, w_dq, q_norm, w_uq, w_o, w_up, w_down, ple_proj, ple_gate):
    W = _prep_weights(norm_mix_pre, norm_mix_post, norm_ff_pre, norm_ff_post,
                      conv_pw1, conv_dw, conv_dw_bias, conv_ln_g, conv_ln_b, conv_pw2,
                      kv_in_norm, w_dkv, kv_norm, w_uk, w_uv, w_dq, q_norm, w_uq, w_o,
                      w_up, w_down, ple_proj, ple_gate)
    b, t, d = x_prompt.shape
    db, ts, _ = x_sample.shape
    depth = norm_mix_pre.shape[0]
    n_a = conv_pw1.shape[0]
    n_b = depth - n_a
    width = conv_dw.shape[1]
    n_heads, rope_dim = W['n_heads'], W['rope_dim']
    page = cache_kv_latent.shape[1]
    past_len = page_table.shape[1] * page
    tm, tm_p = 256, 512
    hist_pad = ((0, 0), (CONV_PAD - (width - 1), 0), (0, 0))

    cos_p, sin_p = _rope_tables(jnp.arange(t), rope_dim)
    ple_p = p_prompt.reshape(depth, b * t, -1)
    x = x_prompt
    conv_p = []
    zero_prev = jnp.zeros((b, CONV_PAD, d), F32)
    for i in range(n_a):
        x, st = _conv_layer_prompt(x, ple_p, zero_prev, W, i)
        conv_p.append(st[:, CONV_PAD - (width - 1):, :])
    x = x.reshape(b * t, d)
    c_p, kr_p, kcat, v = _kv_call(x, cos_p, sin_p, W, up=True, tm=tm_p)
    v = v.reshape(b, t, -1)
    for j in range(n_b):
        i = n_a + j
        (qcat,) = _q_call(x, cos_p, sin_p, W, i, j, absorb=False, tm=tm_p)
        o = _prompt_attn(qcat.reshape(b, t, -1), kcat, v, n_heads)
        x = _post_call(x, o.reshape(b * t, -1), ple_p, W, i, j, latent=False, tm=tm_p)
    y_prompt = x.reshape(b, t, d)

    pos_s = past_len + jnp.arange(ts)
    cos_s, sin_s = _rope_tables(jnp.tile(pos_s, tm // ts), rope_dim)
    ple_s = p_sample.reshape(depth, db * ts, -1)
    x = x_sample.reshape(db * ts, d)
    conv_s = []
    for i in range(n_a):
        x, st = _conv_layer_sample(x, ple_s, jnp.pad(state_conv[i], hist_pad), W, i, ts)
        conv_s.append(st[:, CONV_PAD - (width - 1):, :])
    c_s, kr_s = _kv_call(x, cos_s, sin_s, W, up=False, tm=tm)
    cache_krt = jnp.swapaxes(cache_k_rope, 1, 2)
    for j in range(n_b):
        i = n_a + j
        qlat, qrope = _q_call(x, cos_s, sin_s, W, i, j, absorb=True, tm=tm)
        o_lat = _sample_attn(qlat, qrope, c_s, kr_s, cache_kv_latent, cache_krt, page_table, ts)
        x = _post_call(x, o_lat, ple_s, W, i, j, latent=True, tm=tm)
    y_sample = x.reshape(db, ts, d)

    return (y_prompt, y_sample, jnp.stack(conv_p), jnp.stack(conv_s),
            c_p.reshape(b, t, -1), kr_p.reshape(b, t, -1),
            c_s.reshape(db, ts, -1), kr_s.reshape(db, ts, -1))
```

```python
import functools

import jax
import jax.numpy as jnp
from jax import lax
from jax.experimental import pallas as pl
from jax.experimental.pallas import tpu as pltpu

F32 = jnp.float32
BF16 = jnp.bfloat16

EPS = 1e-6
ROPE_THETA = 10000.0
LANES = 128
VMEM_LIMIT = 56 * 1024 * 1024

CONV_PAD = 32
FF_CHUNK = 1024
PAGE_SLOTS = 3


def _rms(x, g):
    return x * lax.rsqrt(jnp.mean(x * x, axis=-1, keepdims=True) + EPS) * g


def _mm(a, w):
    return jnp.dot(a.astype(BF16), w, preferred_element_type=F32)


def _dot_nt(a, b):
    return lax.dot_general(a, b, (((1,), (1,)), ((), ())), preferred_element_type=F32)


def _rope_chunk(chunk, cos_t, sin_t):
    half = chunk.shape[-1] // 4
    lane = lax.broadcasted_iota(jnp.int32, chunk.shape, 1)
    swapped = jnp.where(lane < half,
                        pltpu.roll(chunk, LANES - half, 1),
                        pltpu.roll(chunk, half, 1))
    return chunk * cos_t + swapped * sin_t


def _ffn_ple(x, ple, g_ffpre, g_ffpost, wup, wdown, wgate, wproj):
    d_ff = wup.shape[1]
    h = _rms(x, g_ffpre).astype(BF16)
    acc = jnp.zeros_like(x)
    for c in range(d_ff // FF_CHUNK):
        u = jnp.dot(h, wup[:, c * FF_CHUNK:(c + 1) * FF_CHUNK], preferred_element_type=F32)
        u = jnp.square(jnp.maximum(u, 0.0)).astype(BF16)
        acc = acc + jnp.dot(u, wdown[c * FF_CHUNK:(c + 1) * FF_CHUNK, :],
                            preferred_element_type=F32)
    x = x + _rms(acc, g_ffpost)
    gate = jax.nn.sigmoid(_mm(x, wgate[...]))
    return x + gate * _mm(ple, wproj[...])


def _layer_norm_silu(y, g, b):
    mu = jnp.mean(y, axis=-1, keepdims=True)
    yc = y - mu
    var = jnp.mean(yc * yc, axis=-1, keepdims=True)
    z = yc * lax.rsqrt(var + EPS) * g + b
    return z * jax.nn.sigmoid(z)


def _conv_prompt_kernel(x_ref, ple_ref, prev_ref, gpre, gpost, gffpre, gffpost,
                        pw1, dw, dwb, lng, lnb, pw2, wup, wdown, wgate, wproj,
                        xo_ref, st_ref, buf, ybuf, sh, xm_prev, *, tt, nt, width):
    g = pl.program_id(0)
    n_tiles = pl.num_programs(0) - 1
    t = jnp.minimum(g, n_tiles - 1) % nt
    d = x_ref.shape[-1]

    @pl.when(g == 0)
    def _():
        xm_prev[...] = jnp.zeros(xm_prev.shape, F32)

    @pl.when(t == 0)
    def _():
        buf[0:CONV_PAD, :] = prev_ref[0]

    x = x_ref[0]
    h = _rms(x, gpre[...])
    ag = _mm(h, pw1[...])
    buf[CONV_PAD:CONV_PAD + tt, :] = ag[:, :d] * jax.nn.sigmoid(ag[:, d:])
    off = CONV_PAD - (width - 1)
    n_sh = sh.shape[1]
    for s in range(1, 8):
        sh[s - 1] = buf[s:s + n_sh, :]
    rc, lc = 64, 256

    def conv_rows(r):
        for c in range(d // lc):
            cs = slice(c * lc, (c + 1) * lc)
            acc = jnp.broadcast_to(dwb[:, cs], (rc, lc))
            for k in range(width):
                s = (k + off) % 8
                lo = r * rc + (k + off) - s
                tap = buf[lo:lo + rc, cs] if s == 0 else sh[s - 1, lo:lo + rc, cs]
                acc = acc + tap * dw[k:k + 1, cs]
            ybuf[r * rc:(r + 1) * rc, cs] = acc

    xp = xm_prev[...]
    hp = _rms(xp, gffpre[...]).astype(BF16)
    n_ff = wup.shape[1] // FF_CHUNK
    n_rb = tt // rc
    acc_ff = jnp.zeros_like(xp)
    for c in range(n_ff):
        u = jnp.dot(hp, wup[:, c * FF_CHUNK:(c + 1) * FF_CHUNK], preferred_element_type=F32)
        u = jnp.square(jnp.maximum(u, 0.0)).astype(BF16)
        acc_ff = acc_ff + jnp.dot(u, wdown[c * FF_CHUNK:(c + 1) * FF_CHUNK, :],
                                  preferred_element_type=F32)
        for r in range(c * n_rb // n_ff, (c + 1) * n_rb // n_ff):
            conv_rows(r)
    xp = xp + _rms(acc_ff, gffpost[...])
    gate = jax.nn.sigmoid(_mm(xp, wgate[...]))
    xo_ref[0] = xp + gate * _mm(ple_ref[...], wproj[...])

    y = _layer_norm_silu(ybuf[...], lng[...], lnb[...])
    xm_prev[...] = x + _rms(_mm(y, pw2[...]), gpost[...])

    tail = buf[tt:tt + CONV_PAD, :]
    buf[0:CONV_PAD, :] = tail

    @pl.when(t == nt - 1)
    def _():
        st_ref[0] = tail


def _conv_sample_kernel(x_ref, ple_ref, prev_ref, gpre, gpost, gffpre, gffpost,
                        pw1, dw, dwb, lng, lnb, pw2, wup, wdown, wgate, wproj,
                        xo_ref, st_ref, full, ybuf):
    ts, bb, d = x_ref.shape
    hist = prev_ref.shape[0]
    width = dw.shape[0]
    x = x_ref[...].reshape(ts * bb, d)
    h = _rms(x, gpre[...])
    ag = _mm(h, pw1[...])
    full[0:hist] = prev_ref[...]
    full[hist:hist + ts] = (ag[:, :d] * jax.nn.sigmoid(ag[:, d:])).reshape(ts, bb, d)

    lc = 256
    for t in range(ts):
        for c in range(d // lc):
            cs = slice(c * lc, (c + 1) * lc)
            acc = jnp.broadcast_to(dwb[:, cs], (bb, lc))
            for k in range(width):
                acc = acc + full[t + k, :, cs] * dw[k:k + 1, cs]
            ybuf[t, :, cs] = acc

    y = _layer_norm_silu(ybuf[...].reshape(ts * bb, d), lng[...], lnb[...])
    x = x + _rms(_mm(y, pw2[...]), gpost[...])
    ple = ple_ref[...].reshape(ts * bb, ple_ref.shape[-1])
    xo_ref[...] = _ffn_ple(x, ple, gffpre[...], gffpost[...],
                           wup, wdown, wgate, wproj).reshape(ts, bb, d)
    st_ref[...] = full[ts:ts + hist]


def _kv_kernel(x_ref, cos_ref, sin_ref, gin, gkv, wdkv, *rest, kv_lora, rope_dim, n_heads, up):
    if up:
        wuk, wuv, c_ref, kr_ref, kcat_ref, v_ref = rest
    else:
        c_ref, kr_ref = rest
    h = _rms(x_ref[...], gin[...])
    ckr = _mm(h, wdkv[...])
    c = _rms(ckr[:, :kv_lora], gkv[...])
    c_ref[...] = c
    kr = _rope_chunk(ckr[:, kv_lora:kv_lora + LANES], cos_ref[...], sin_ref[...])
    kr_ref[...] = kr[:, :rope_dim]
    if up:
        cb = c.astype(BF16)
        kn = jnp.dot(cb, wuk[...], preferred_element_type=F32)
        v_ref[...] = jnp.dot(cb, wuv[...], preferred_element_type=F32).astype(BF16)
        nope = kn.shape[1] // n_heads
        krt = kr.T.astype(BF16)
        for hh in range(n_heads):
            base = hh * (nope + LANES)
            kcat_ref[base:base + nope, :] = kn[:, hh * nope:(hh + 1) * nope].T.astype(BF16)
            kcat_ref[base + nope:base + nope + LANES, :] = krt


def _q_kernel(x_ref, cos_ref, sin_ref, gpre, gq, wdq, wuq, *rest, n_heads, nope, scale, absorb,
              n_split):
    tm = x_ref.shape[0]
    rb = tm // n_split
    blocks = [slice(k * rb, (k + 1) * rb) for k in range(n_split)]
    hs = [_rms(x_ref[r, :], gpre[...]) for r in blocks]
    cqs = [_rms(_mm(h, wdq[...]), gq[...]) for h in hs]
    qs = [_mm(cq, wuq[...]) * scale for cq in cqs]
    if absorb:
        wukt, qlat_ref, qrope_ref = rest
    else:
        (qcat_ref,) = rest
    for hh in range(n_heads):
        for r, q in zip(blocks, qs):
            ro = n_heads * nope + hh * LANES
            rope = _rope_chunk(q[:, ro:ro + LANES], cos_ref[r, :], sin_ref[r, :])
            if absorb:
                qlat_ref[hh, r, :] = _mm(q[:, hh * nope:(hh + 1) * nope], wukt[hh])
                qrope_ref[hh, r, :] = rope
            else:
                base = hh * (nope + LANES)
                qcat_ref[r, base:base + nope] = q[:, hh * nope:(hh + 1) * nope].astype(BF16)
                qcat_ref[r, base + nope:base + nope + LANES] = rope.astype(BF16)


def _prompt_attn_kernel(q_ref, k_ref, v_ref, o_ref, *, tq):
    t = q_ref.shape[1]
    row = lax.broadcasted_iota(jnp.int32, (tq, tq), 0)
    col = lax.broadcasted_iota(jnp.int32, (tq, tq), 1)
    causal = col <= row
    n_q = t // tq

    def scores(qi):
        lo = qi * tq
        q = q_ref[0, lo:lo + tq, :]
        s_d = jnp.where(causal, jnp.dot(q, k_ref[:, lo:lo + tq], preferred_element_type=F32),
                        -jnp.inf)
        s_o = jnp.dot(q, k_ref[:, 0:lo], preferred_element_type=F32) if qi > 0 else None
        return s_d, s_o

    def softmax(qi, s_d, s_o):
        m = jnp.max(s_d, axis=-1, keepdims=True)
        if qi > 0:
            m = jnp.maximum(m, jnp.max(s_o, axis=-1, keepdims=True))
        p_d = jnp.exp(s_d - m)
        l = jnp.sum(p_d, axis=-1, keepdims=True)
        p_o = None
        if qi > 0:
            p_o = jnp.exp(s_o - m)
            l = l + jnp.sum(p_o, axis=-1, keepdims=True)
            p_o = p_o.astype(BF16)
        return p_d.astype(BF16), p_o, l

    def weighted_values(qi, p_d, p_o, l):
        lo = qi * tq
        o = jnp.dot(p_d, v_ref[0, lo:lo + tq, :], preferred_element_type=F32)
        if qi > 0:
            o = o + jnp.dot(p_o, v_ref[0, 0:lo, :], preferred_element_type=F32)
        o_ref[0, lo:lo + tq, :] = (o / l).astype(o_ref.dtype)

    nxt = scores(0)
    pending = None
    for qi in range(n_q):
        s_d, s_o = nxt
        if qi + 1 < n_q:
            nxt = scores(qi + 1)
        probs = softmax(qi, s_d, s_o)
        if pending is not None:
            weighted_values(qi - 1, *pending)
        pending = probs
    weighted_values(n_q - 1, *pending)


def _sample_attn_kernel(pt_ref, qlat_ref, qrope_ref, cnew_ref, krnew_ref, cache_c, cache_krt,
                        o_ref, m_s, l_s, acc_s, qt_s, qrb_s, cbuf, pad_s, kpad_s, cf, kf, sem,
                        *, kp, sub, ts, rope_dim):
    j = pl.program_id(1)
    nj = pl.num_programs(1)
    step = pl.program_id(0) * nj + j
    n_steps = pl.num_programs(0) * nj
    n_slots = cf.shape[0]
    slot = step % n_slots
    n_heads = qlat_ref.shape[0]
    rows = n_heads * ts
    page = cf.shape[2]

    def page_copies(step_idx, slot_idx):
        copies = []
        for i in range(kp):
            pid = pt_ref[step_idx * kp + i]
            copies.append(pltpu.make_async_copy(cache_c.at[pid], cf.at[slot_idx, i],
                                                sem.at[slot_idx]))
            copies.append(pltpu.make_async_copy(cache_krt.at[pid], kf.at[slot_idx, i],
                                                sem.at[slot_idx]))
        return copies

    ahead = n_slots - 1

    def start_fetch(step_idx):
        for n, cp in enumerate(page_copies(step_idx, step_idx % n_slots)):
            cp.start(priority=(n // 2) % 2)

    @pl.when(step == 0)
    def _():
        for s0 in range(ahead):
            @pl.when(s0 < n_steps)
            def _():
                start_fetch(s0)

    @pl.when(step + ahead < n_steps)
    def _():
        start_fetch(step + ahead)

    @pl.when(j == 0)
    def _():
        m_s[...] = jnp.full(m_s.shape, -jnp.inf, F32)
        l_s[...] = jnp.zeros(l_s.shape, F32)
        acc_s[...] = jnp.zeros(acc_s.shape, F32)
        pad_s[...] = jnp.zeros(pad_s.shape, F32)
        pad_s[0:rows, :] = qlat_ref[...].reshape(rows, qlat_ref.shape[-1])
        qt_s[...] = pad_s[...].T.astype(BF16)
        qrb_s[...] = qrope_ref[...].reshape(rows, LANES)[:, :rope_dim].astype(BF16)

    for cp in page_copies(step, slot):
        cp.wait()

    qrb = qrb_s[...]

    def fold(state, s, cb):
        m_old, l, acc = state
        m_new = jnp.maximum(m_old, jnp.max(s, axis=-1, keepdims=True))
        alpha = jnp.exp(m_old - m_new)
        p = jnp.exp(s - m_new)
        l = alpha * l + jnp.sum(p, axis=-1, keepdims=True)
        acc = alpha * acc + jnp.dot(p.astype(BF16), cb, preferred_element_type=F32)
        return m_new, l, acc

    def scores(cb, rope_scores):
        s_t = jnp.dot(cb, qt_s[...], preferred_element_type=F32)
        return s_t.T[:rows, :] + rope_scores

    state = (m_s[...], l_s[...], acc_s[...])
    for i in range(kp):
        cbuf[i * page:(i + 1) * page, :] = cf[slot, i].astype(BF16)
    gk = sub * page
    s_parts = []
    for g in range(kp // sub):
        cb = cbuf[g * gk:(g + 1) * gk, :]
        s_rope = jnp.concatenate(
            [jnp.dot(qrb, kf[slot, i].astype(BF16), preferred_element_type=F32)
             for i in range(g * sub, (g + 1) * sub)], axis=1)
        s_parts.append(scores(cb, s_rope))
    m_old, l, acc = state
    m_new = m_old
    for s in s_parts:
        m_new = jnp.maximum(m_new, jnp.max(s, axis=-1, keepdims=True))
    alpha = jnp.exp(m_old - m_new)
    l = alpha * l
    acc = alpha * acc
    for g, s in enumerate(s_parts):
        p = jnp.exp(s - m_new)
        l = l + jnp.sum(p, axis=-1, keepdims=True)
        acc = acc + jnp.dot(p.astype(BF16), cbuf[g * gk:(g + 1) * gk, :],
                            preferred_element_type=F32)
    state = (m_new, l, acc)

    @pl.when(j < nj - 1)
    def _():
        m_s[...], l_s[...], acc_s[...] = state

    @pl.when(j == nj - 1)
    def _():
        pad_s[...] = jnp.zeros(pad_s.shape, F32)
        kpad_s[...] = jnp.zeros(kpad_s.shape, F32)
        pad_s[0:ts, :] = cnew_ref[...]
        kpad_s[0:ts, :] = krnew_ref[...]
        cb = pad_s[...].astype(BF16)
        s = scores(cb, _dot_nt(qrb, kpad_s[...].astype(BF16)))
        t_q = lax.broadcasted_iota(jnp.int32, s.shape, 0) % ts
        t_k = lax.broadcasted_iota(jnp.int32, s.shape, 1)
        _, l, acc = fold(state, jnp.where(t_k <= t_q, s, -jnp.inf), cb)
        o_ref[...] = (acc / l).reshape(o_ref.shape)


def _post_kernel(x_ref, o_ref, ple_ref, gpost, gffpre, gffpost, wo, *rest, latent):
    if latent:
        wuv, wup, wdown, wgate, wproj, xo_ref = rest
        n_heads, _, vh = wuv.shape
        m = jnp.zeros(x_ref.shape, F32)
        for hh in range(n_heads):
            oh = _mm(o_ref[hh], wuv[hh])
            m = m + _mm(oh, wo[hh * vh:(hh + 1) * vh, :])
    else:
        wup, wdown, wgate, wproj, xo_ref = rest
        m = jnp.dot(o_ref[...], wo[...], preferred_element_type=F32)
    x = x_ref[...] + _rms(m, gpost[...])
    xo_ref[...] = _ffn_ple(x, ple_ref[...], gffpre[...], gffpost[...], wup, wdown, wgate, wproj)


def _wspec(w):
    if isinstance(w, tuple):
        arr, layer = w
        rest = arr.ndim - 1
        return pl.BlockSpec((None,) + arr.shape[1:], lambda *_: (layer,) + (0,) * rest,
                            pipeline_mode=pl.Buffered(1))
    nd = w.ndim
    return pl.BlockSpec(w.shape, lambda *_: (0,) * nd, pipeline_mode=pl.Buffered(1))


def _warg(w):
    return w[0] if isinstance(w, tuple) else w


def _params(sem):
    return pltpu.CompilerParams(dimension_semantics=sem, vmem_limit_bytes=VMEM_LIMIT)


def _conv_weights(W, i):
    names = ['g_pre', 'g_post', 'g_ffpre', 'g_ffpost', 'pw1', 'dw', 'dwb', 'lng', 'lnb', 'pw2',
             'w_up', 'w_down', 'ple_gate', 'ple_proj']
    return [(W[n], i) for n in names]


def _conv_layer_prompt(x, ple, prev, W, i, tt=256):
    b, t, d = x.shape
    nt = t // tt
    n_tiles = b * nt
    ws = _conv_weights(W, i)
    width = W['dw'].shape[1]

    def mixer_tile(g):
        return jnp.minimum(g, n_tiles - 1)

    def tail_tile(g):
        return jnp.maximum(g - 1, 0)

    return pl.pallas_call(
        functools.partial(_conv_prompt_kernel, tt=tt, nt=nt, width=width),
        grid=(n_tiles + 1,),
        in_specs=[pl.BlockSpec((1, tt, d), lambda g: (mixer_tile(g) // nt, mixer_tile(g) % nt, 0)),
                  pl.BlockSpec((None, tt, ple.shape[-1]), lambda g: (i, tail_tile(g), 0)),
                  pl.BlockSpec((1, CONV_PAD, d), lambda g: (mixer_tile(g) // nt, 0, 0))]
                 + [_wspec(w) for w in ws],
        out_specs=[pl.BlockSpec((1, tt, d), lambda g: (tail_tile(g) // nt, tail_tile(g) % nt, 0)),
                   pl.BlockSpec((1, CONV_PAD, d), lambda g: (mixer_tile(g) // nt, 0, 0))],
        out_shape=[jax.ShapeDtypeStruct((b, t, d), F32),
                   jax.ShapeDtypeStruct((b, CONV_PAD, d), F32)],
        scratch_shapes=[pltpu.VMEM((CONV_PAD + tt, d), F32), pltpu.VMEM((tt, d), F32),
                        pltpu.VMEM((7, tt + CONV_PAD - 8, d), F32),
                        pltpu.VMEM((tt, d), F32)],
        compiler_params=_params(("arbitrary",)),
        name=f"conv_layer_prompt_{i}",
    )(x, ple, prev, *[_warg(w) for w in ws])


def _conv_layer_sample(x_t, ple_t, state_t, W, i, bb=32):
    ts, db, d = x_t.shape
    hist = state_t.shape[1]
    ws = _conv_weights(W, i)
    return pl.pallas_call(
        _conv_sample_kernel,
        grid=(db // bb,),
        in_specs=[pl.BlockSpec((ts, bb, d), lambda g: (0, g, 0)),
                  pl.BlockSpec((None, ts, bb, ple_t.shape[-1]), lambda g: (i, 0, g, 0)),
                  pl.BlockSpec((None, hist, bb, d), lambda g: (i, 0, g, 0))]
                 + [_wspec(w) for w in ws],
        out_specs=[pl.BlockSpec((ts, bb, d), lambda g: (0, g, 0)),
                   pl.BlockSpec((hist, bb, d), lambda g: (0, g, 0))],
        out_shape=[jax.ShapeDtypeStruct((ts, db, d), F32),
                   jax.ShapeDtypeStruct((hist, db, d), F32)],
        scratch_shapes=[pltpu.VMEM((hist + ts, bb, d), F32), pltpu.VMEM((ts, bb, d), F32)],
        compiler_params=_params(("parallel",)),
        name=f"conv_layer_sample_{i}",
    )(x_t, ple_t, state_t, *[_warg(w) for w in ws])


def _table_spec(tab, tm):
    nblk = tab.shape[0] // tm
    return pl.BlockSpec((tm, LANES), lambda g: (g % nblk, 0))


def _kv_call(x, cos_t, sin_t, W, up, tm=256):
    n, d = x.shape
    kv_lora = W['g_kv'].shape[-1]
    rope_dim = W['rope_dim']
    n_heads = W['n_heads']
    ws = [W['g_kvin'], W['g_kv'], W['w_dkv']] + ([W['w_uk_flat'], W['w_uv_flat']] if up else [])
    out_specs = [pl.BlockSpec((tm, kv_lora), lambda g: (g, 0)),
                 pl.BlockSpec((tm, rope_dim), lambda g: (g, 0))]
    out_shape = [jax.ShapeDtypeStruct((n, kv_lora), F32),
                 jax.ShapeDtypeStruct((n, rope_dim), F32)]
    if up:
        kw = W['w_uk_flat'].shape[1] + n_heads * LANES
        vw = W['w_uv_flat'].shape[1]
        out_specs += [pl.BlockSpec((kw, tm), lambda g: (0, g)),
                      pl.BlockSpec((tm, vw), lambda g: (g, 0))]
        out_shape += [jax.ShapeDtypeStruct((kw, n), BF16), jax.ShapeDtypeStruct((n, vw), BF16)]
    return pl.pallas_call(
        functools.partial(_kv_kernel, kv_lora=kv_lora, rope_dim=rope_dim, n_heads=n_heads, up=up),
        grid=(n // tm,),
        in_specs=[pl.BlockSpec((tm, d), lambda g: (g, 0)),
                  _table_spec(cos_t, tm), _table_spec(sin_t, tm)] + [_wspec(w) for w in ws],
        out_specs=out_specs,
        out_shape=out_shape,
        compiler_params=_params(("parallel",)),
        name="shared_kv_up" if up else "shared_kv",
    )(x, cos_t, sin_t, *[_warg(w) for w in ws])


def _q_call(x, cos_t, sin_t, W, i, j, absorb, tm=256):
    n, d = x.shape
    n_heads, nope = W['n_heads'], W['nope']
    ws = [(W['g_pre'], i), (W['g_q'], j), (W['w_dq'], j), (W['w_uq'], j)]
    if absorb:
        ws.append(W['w_ukt'])
        kv_lora = W['w_ukt'].shape[-1]
        out_specs = [pl.BlockSpec((n_heads, tm, kv_lora), lambda g: (0, g, 0)),
                     pl.BlockSpec((n_heads, tm, LANES), lambda g: (0, g, 0))]
        out_shape = [jax.ShapeDtypeStruct((n_heads, n, kv_lora), F32),
                     jax.ShapeDtypeStruct((n_heads, n, LANES), F32)]
    else:
        qw = n_heads * (nope + LANES)
        out_specs = [pl.BlockSpec((tm, qw), lambda g: (g, 0))]
        out_shape = [jax.ShapeDtypeStruct((n, qw), BF16)]
    return pl.pallas_call(
        functools.partial(_q_kernel, n_heads=n_heads, nope=nope, scale=W['sm_scale'],
                          absorb=absorb, n_split=2),
        grid=(n // tm,),
        in_specs=[pl.BlockSpec((tm, d), lambda g: (g, 0)),
                  _table_spec(cos_t, tm), _table_spec(sin_t, tm)] + [_wspec(w) for w in ws],
        out_specs=out_specs,
        out_shape=out_shape,
        compiler_params=_params(("parallel",)),
        name=f"mla_query_{'sample' if absorb else 'prompt'}_{j}",
    )(x, cos_t, sin_t, *[_warg(w) for w in ws])


def _prompt_attn(qcat, kcat_t, v, n_heads, tq=256):
    b, t, qw = qcat.shape
    hw = qw // n_heads
    vh = v.shape[-1] // n_heads
    return pl.pallas_call(
        functools.partial(_prompt_attn_kernel, tq=tq),
        grid=(b, n_heads),
        in_specs=[pl.BlockSpec((1, t, hw), lambda bi, hi: (bi, 0, hi)),
                  pl.BlockSpec((hw, t), lambda bi, hi: (hi, bi)),
                  pl.BlockSpec((1, t, vh), lambda bi, hi: (bi, 0, hi))],
        out_specs=pl.BlockSpec((1, t, vh), lambda bi, hi: (bi, 0, hi)),
        out_shape=jax.ShapeDtypeStruct((b, t, n_heads * vh), BF16),
        compiler_params=_params(("parallel", "parallel")),
        name="prompt_attention",
    )(qcat, kcat_t, v)


def _sample_attn(qlat, qrope, cnew, krnew, cache_c, cache_krt, page_table, ts, kp=32, sub=16):
    n_heads, n, kv_lora = qlat.shape
    db, n_pages = page_table.shape
    _, rope_dim, page = cache_krt.shape
    nchunk = n_pages // kp
    rows = n_heads * ts
    pt_flat = page_table.reshape(-1)

    in_specs = [pl.BlockSpec((n_heads, ts, kv_lora), lambda b, j, pt: (0, b, 0)),
                pl.BlockSpec((n_heads, ts, LANES), lambda b, j, pt: (0, b, 0)),
                pl.BlockSpec((ts, kv_lora), lambda b, j, pt: (b, 0)),
                pl.BlockSpec((ts, rope_dim), lambda b, j, pt: (b, 0)),
                pl.BlockSpec(memory_space=pl.ANY), pl.BlockSpec(memory_space=pl.ANY)]
    grid_spec = pltpu.PrefetchScalarGridSpec(
        num_scalar_prefetch=1,
        grid=(db, nchunk),
        in_specs=in_specs,
        out_specs=pl.BlockSpec((n_heads, ts, kv_lora), lambda b, j, pt: (0, b, 0)),
        scratch_shapes=[pltpu.VMEM((rows, 1), F32), pltpu.VMEM((rows, 1), F32),
                        pltpu.VMEM((rows, kv_lora), F32),
                        pltpu.VMEM((kv_lora, LANES), BF16), pltpu.VMEM((rows, rope_dim), BF16),
                        pltpu.VMEM((kp * page, kv_lora), BF16),
                        pltpu.VMEM((LANES, kv_lora), F32), pltpu.VMEM((LANES, rope_dim), F32),
                        pltpu.VMEM((PAGE_SLOTS, kp, page, kv_lora), F32),
                        pltpu.VMEM((PAGE_SLOTS, kp, rope_dim, page), F32),
                        pltpu.SemaphoreType.DMA((PAGE_SLOTS,))],
    )
    return pl.pallas_call(
        functools.partial(_sample_attn_kernel, kp=kp, sub=sub, ts=ts, rope_dim=rope_dim),
        grid_spec=grid_spec,
        out_shape=jax.ShapeDtypeStruct((n_heads, n, kv_lora), F32),
        compiler_params=_params(("arbitrary", "arbitrary")),
        name="sample_attention",
    )(pt_flat, qlat, qrope, cnew, krnew, cache_c, cache_krt)


def _post_call(x, o, ple, W, i, j, latent, tm=256):
    n, d = x.shape
    ws = [(W['g_post'], i), (W['g_ffpre'], i), (W['g_ffpost'], i), (W['w_o'], j)]
    if latent:
        ws.append(W['w_uv_heads'])
        o_spec = pl.BlockSpec((o.shape[0], tm, o.shape[2]), lambda g: (0, g, 0))
    else:
        o_spec = pl.BlockSpec((tm, o.shape[1]), lambda g: (g, 0))
    ws += [(W['w_up'], i), (W['w_down'], i), (W['ple_gate'], i), (W['ple_proj'], i)]
    return pl.pallas_call(
        functools.partial(_post_kernel, latent=latent),
        grid=(n // tm,),
        in_specs=[pl.BlockSpec((tm, d), lambda g: (g, 0)), o_spec,
                  pl.BlockSpec((None, tm, ple.shape[-1]), lambda g: (i, g, 0))]
                 + [_wspec(w) for w in ws],
        out_specs=pl.BlockSpec((tm, d), lambda g: (g, 0)),
        out_shape=jax.ShapeDtypeStruct((n, d), F32),
        compiler_params=_params(("parallel",)),
        name=f"attn_out_mlp_{'sample' if latent else 'prompt'}_{j}",
    )(x, o, ple, *[_warg(w) for w in ws])


def _rope_tables(pos, rope_dim):
    half = rope_dim // 2
    freqs = 1.0 / (ROPE_THETA ** (jnp.arange(half, dtype=F32) / half))
    ang = pos.astype(F32)[:, None] * freqs[None, :]
    cos, sin = jnp.cos(ang), jnp.sin(ang)
    z = jnp.zeros((pos.shape[0], LANES - rope_dim), F32)
    return (jnp.concatenate([cos, cos, z], axis=-1),
            jnp.concatenate([-sin, sin, z], axis=-1))


def _prep_weights(norm_mix_pre, norm_mix_post, norm_ff_pre, norm_ff_post,
                  conv_pw1, conv_dw, conv_dw_bias, conv_ln_g, conv_ln_b, conv_pw2,
                  kv_in_norm, w_dkv, kv_norm, w_uk, w_uv, w_dq, q_norm, w_uq, w_o,
                  w_up, w_down, ple_proj, ple_gate):
    depth, d = norm_mix_pre.shape
    kv_lora, n_heads, nope = w_uk.shape
    vh = w_uv.shape[-1]
    rope_dim = w_dkv.shape[1] - kv_lora
    n_b = w_dq.shape[0]
    q_lora = w_dq.shape[-1]
    row = lambda g: g[:, None, :]
    bf = lambda w: w.astype(BF16)

    uq = w_uq.reshape(n_b, q_lora, n_heads, nope + rope_dim)
    uq_nope = uq[..., :nope].reshape(n_b, q_lora, n_heads * nope)
    uq_rope = jnp.pad(uq[..., nope:], ((0, 0), (0, 0), (0, 0), (0, LANES - rope_dim)))
    uq_cat = jnp.concatenate([uq_nope, uq_rope.reshape(n_b, q_lora, n_heads * LANES)], axis=-1)

    return {
        'n_heads': n_heads, 'nope': nope, 'rope_dim': rope_dim,
        'sm_scale': float((nope + rope_dim) ** -0.5),
        'g_pre': row(norm_mix_pre), 'g_post': row(norm_mix_post),
        'g_ffpre': row(norm_ff_pre), 'g_ffpost': row(norm_ff_post),
        'pw1': bf(conv_pw1), 'dw': conv_dw,
        'dwb': row(conv_dw_bias), 'lng': row(conv_ln_g), 'lnb': row(conv_ln_b),
        'pw2': bf(conv_pw2),
        'g_kvin': kv_in_norm[None, :], 'g_kv': kv_norm[None, :],
        'w_dkv': jnp.pad(w_dkv, ((0, 0), (0, LANES - rope_dim))).astype(BF16),
        'w_uk_flat': w_uk.reshape(kv_lora, n_heads * nope).astype(BF16),
        'w_uv_flat': w_uv.reshape(kv_lora, n_heads * vh).astype(BF16),
        'w_ukt': jnp.transpose(w_uk, (1, 2, 0)).astype(BF16),
        'w_uv_heads': jnp.transpose(w_uv, (1, 0, 2)).astype(BF16),
        'w_dq': bf(w_dq), 'g_q': row(q_norm), 'w_uq': bf(uq_cat), 'w_o': bf(w_o),
        'w_up': bf(w_up), 'w_down': bf(w_down),
        'ple_gate': bf(ple_gate), 'ple_proj': bf(ple_proj),
    }


def kernel(x_prompt, x_sample, state_conv, cache_kv_latent, cache_k_rope, page_table, p_prompt, p_sample, norm_mix_pre, norm_mix_post, norm_ff_pre, norm_ff_post, conv_pw1, conv_dw, conv_dw_bias, conv_ln_g, conv_ln_b, conv_pw2, kv_in_norm, w_dkv, kv_norm, w_uk, w_uv, w_dq, q_norm, w_uq, w_o, w_up, w_down, ple_proj, ple_gate):
    W = _prep_weights(norm_mix_pre, norm_mix_post, norm_ff_pre, norm_ff_post,
                      conv_pw1, conv_dw, conv_dw_bias, conv_ln_g, conv_ln_b, conv_pw2,
                      kv_in_norm, w_dkv, kv_norm, w_uk, w_uv, w_dq, q_norm, w_uq, w_o,
                      w_up, w_down, ple_proj, ple_gate)
    b, t, d = x_prompt.shape
    db, ts, _ = x_sample.shape
    depth = norm_mix_pre.shape[0]
    n_a = conv_pw1.shape[0]
    n_b = depth - n_a
    width = conv_dw.shape[1]
    n_heads, rope_dim = W['n_heads'], W['rope_dim']
    page = cache_kv_latent.shape[1]
    past_len = page_table.shape[1] * page
    tm, tm_p = 256, 512

    cos_p, sin_p = _rope_tables(jnp.arange(t), rope_dim)
    ple_p = p_prompt.reshape(depth, b * t, -1)
    x = x_prompt
    conv_p = []
    zero_prev = jnp.zeros((b, CONV_PAD, d), F32)
    for i in range(n_a):
        x, st = _conv_layer_prompt(x, ple_p, zero_prev, W, i)
        conv_p.append(st[:, CONV_PAD - (width - 1):, :])
    x = x.reshape(b * t, d)
    c_p, kr_p, kcat, v = _kv_call(x, cos_p, sin_p, W, up=True, tm=tm_p)
    v = v.reshape(b, t, -1)
    for j in range(n_b):
        i = n_a + j
        (qcat,) = _q_call(x, cos_p, sin_p, W, i, j, absorb=False, tm=tm_p)
        o = _prompt_attn(qcat.reshape(b, t, -1), kcat, v, n_heads)
        x = _post_call(x, o.reshape(b * t, -1), ple_p, W, i, j, latent=False, tm=tm_p)
    y_prompt = x.reshape(b, t, d)

    pos_s = past_len + jnp.arange(ts)
    cos_s, sin_s = _rope_tables(jnp.tile(pos_s, tm // ts), rope_dim)
    ple_s = p_sample.reshape(depth, db * ts, -1)
    x_t = jnp.transpose(x_sample, (1, 0, 2))
    ple_t = jnp.transpose(p_sample[:n_a], (0, 2, 1, 3))
    state_t = jnp.transpose(state_conv, (0, 2, 1, 3))
    conv_s = []
    for i in range(n_a):
        x_t, st = _conv_layer_sample(x_t, ple_t, state_t, W, i)
        conv_s.append(st)
    x = jnp.transpose(x_t, (1, 0, 2)).reshape(db * ts, d)
    c_s, kr_s = _kv_call(x, cos_s, sin_s, W, up=False, tm=tm)
    cache_krt = jnp.swapaxes(cache_k_rope, 1, 2)
    for j in range(n_b):
        i = n_a + j
        qlat, qrope = _q_call(x, cos_s, sin_s, W, i, j, absorb=True, tm=tm)
        o_lat = _sample_attn(qlat, qrope, c_s, kr_s, cache_kv_latent, cache_krt, page_table, ts)
        x = _post_call(x, o_lat, ple_s, W, i, j, latent=True, tm=tm)
    y_sample = x.reshape(db, ts, d)

    return (y_prompt, y_sample, jnp.stack(conv_p), jnp.transpose(jnp.stack(conv_s), (0, 2, 1, 3)),
            c_p.reshape(b, t, -1), kr_p.reshape(b, t, -1),
            c_s.reshape(db, ts, -1), kr_s.reshape(db, ts, -1))
```
